```python
import math
import jax, jax.numpy as jnp
from jax import lax
import numpy as np

D_MODEL = 2048
BATCH = 1
SEQ = 16384
DEPTH = 2

F32 = jnp.float32
Q_BLOCK = 128
A_HEADS = 4
A_HEAD_DIM = 128
MOBA_BLOCK = 256
MOBA_TOPK = 3
B_HEADS = 8
B_KV_HEADS = 2
B_HEAD_DIM = 64
SWA_WINDOW = 128
C_HEADS = 4
C_HEAD_DIM = 64
D_HEADS = 4
D_Q_LORA = 512
D_KV_LORA = 256
D_NOPE = 128
D_ROPE = 64
D_V = 128
ROPE_THETA = 10000.0
N_BRANCHES = 4
BRANCH_W = 512
T5_BUCKETS = 32
T5_MAX_EXACT = T5_BUCKETS // 2
T5_MAX_DISTANCE = 1024
BIAS_HEADS = A_HEADS + B_HEADS + C_HEADS
PEER_HEADS = 8
PEER_N_KEYS = 128
PEER_N_EXPERTS = PEER_N_KEYS * PEER_N_KEYS
PEER_D_KEY = 128
PEER_TOPK = 16
PEER_TOKEN_CHUNK = 128
DEEPNORM_ALPHA = (2.0 * DEPTH) ** 0.25
DEEPNORM_BETA = (8.0 * DEPTH) ** -0.25
LN_EPS = 1e-5
RMS_EPS = 1e-6
IN_WIDTHS = (A_HEADS * A_HEAD_DIM, A_HEADS * A_HEAD_DIM, A_HEADS * A_HEAD_DIM,
             B_HEADS * B_HEAD_DIM, B_KV_HEADS * B_HEAD_DIM, B_KV_HEADS * B_HEAD_DIM,
             C_HEADS * 2 * C_HEAD_DIM, C_HEADS * 2 * C_HEAD_DIM, C_HEADS * 2 * C_HEAD_DIM,
             D_Q_LORA, D_KV_LORA, D_ROPE)
IN_W = sum(IN_WIDTHS)
IN_SPLITS = tuple(int(v) for v in np.cumsum(IN_WIDTHS)[:-1])

kernel_name = 'hybrid_moba_swa_diff_mla_peer_deepnorm'


def layer_norm(x, g, b):
    xf = x.astype(F32)
    mu = jnp.mean(xf, -1, keepdims=True)
    var = jnp.mean(jnp.square(xf - mu), -1, keepdims=True)
    return ((xf - mu) * lax.rsqrt(var + LN_EPS) * g.astype(F32) + b.astype(F32)).astype(x.dtype)


def rms_norm(x, g):
    xf = x.astype(F32)
    return (xf * lax.rsqrt(jnp.mean(xf * xf, -1, keepdims=True) + RMS_EPS) * g.astype(F32)).astype(x.dtype)


def rope(x, pos):
    half = x.shape[-1] // 2
    inv = ROPE_THETA ** (-jnp.arange(half, dtype=F32) / half)
    ang = pos.astype(F32)[..., None] * inv
    ang = ang.reshape(ang.shape[:2] + (1,) * (x.ndim - 3) + (half,))
    cos, sin = jnp.cos(ang), jnp.sin(ang)
    x1, x2 = x[..., :half].astype(F32), x[..., half:].astype(F32)
    return jnp.concatenate([x1 * cos - x2 * sin, x1 * sin + x2 * cos], -1).astype(x.dtype)


def t5_bucket(rel):
    n = jnp.maximum(rel, 0)
    nf = jnp.maximum(n, 1).astype(F32)
    large = T5_MAX_EXACT + (jnp.log(nf / T5_MAX_EXACT) / math.log(T5_MAX_DISTANCE / T5_MAX_EXACT)
                            * (T5_BUCKETS - T5_MAX_EXACT)).astype(jnp.int32)
    large = jnp.minimum(large, T5_BUCKETS - 1)
    return jnp.where(n < T5_MAX_EXACT, n, large)


def t5_bias(tab, bucket_bh):
    H = tab.shape[1]
    h_idx = jnp.arange(H).reshape((1, H) + (1,) * (bucket_bh.ndim - 2))
    return tab.T[h_idx, bucket_bh].astype(F32)


def moba_attention(q, k, v, pos, tab):
    B, S, H, dh = q.shape
    n_blk = -(-S // MOBA_BLOCK)
    pad = n_blk * MOBA_BLOCK - S
    kp = jnp.pad(k, ((0, 0), (0, pad), (0, 0), (0, 0)))
    vp = jnp.pad(v, ((0, 0), (0, pad), (0, 0), (0, 0)))
    pp = jnp.pad(pos, ((0, 0), (0, pad)))
    k_blk = kp.reshape(B, n_blk, MOBA_BLOCK, H, dh).transpose(0, 3, 1, 2, 4)
    v_blk = vp.reshape(B, n_blk, MOBA_BLOCK, H, dh).transpose(0, 3, 1, 2, 4)
    p_blk = pp.reshape(B, n_blk, MOBA_BLOCK)
    k_mean = jnp.mean(k_blk.astype(F32), axis=3)
    topk = min(MOBA_TOPK, n_blk)
    scale = dh ** -0.5
    gather_blocks = jax.vmap(jax.vmap(lambda t, i: t[i]))
    gather_pos = jax.vmap(lambda t, i: t[i])

    def chunk(c):
        q0 = c * Q_BLOCK
        own = q0 // MOBA_BLOCK
        qc = lax.dynamic_slice_in_dim(q, q0, Q_BLOCK, axis=1).transpose(0, 2, 1, 3)
        pq = lax.dynamic_slice_in_dim(pos, q0, Q_BLOCK, axis=1)
        gate = jnp.einsum('bhqd,bhnd->bhqn', qc.astype(F32), k_mean)
        gate = jnp.where(jnp.arange(n_blk) < own, gate, -jnp.inf)
        _, idx = lax.top_k(gate, topk)
        sel_valid = idx < own
        k_sel = gather_blocks(k_blk, idx)
        v_sel = gather_blocks(v_blk, idx)
        pk_sel = gather_pos(p_blk, idx)
        s_sel = jnp.einsum('bhqd,bhqnkd->bhqnk', qc, k_sel).astype(F32) * scale
        b_sel = t5_bias(tab, t5_bucket(pq[:, None, :, None, None] - pk_sel))
        s_sel = jnp.where(sel_valid[..., None], s_sel + b_sel, -jnp.inf)
        k_own = lax.dynamic_slice_in_dim(kp, own * MOBA_BLOCK, MOBA_BLOCK, axis=1)
        v_own = lax.dynamic_slice_in_dim(vp, own * MOBA_BLOCK, MOBA_BLOCK, axis=1)
        p_own = lax.dynamic_slice_in_dim(pp, own * MOBA_BLOCK, MOBA_BLOCK, axis=1)
        s_own = jnp.einsum('bhqd,bkhd->bhqk', qc, k_own).astype(F32) * scale
        b_own = t5_bias(tab, t5_bucket(pq[:, None, :, None] - p_own[:, None, None, :]))
        qi = q0 + jnp.arange(Q_BLOCK)
        ki = own * MOBA_BLOCK + jnp.arange(MOBA_BLOCK)
        s_own = jnp.where(ki[None, :] <= qi[:, None], s_own + b_own, -jnp.inf)
        n_sel = topk * MOBA_BLOCK
        logits = jnp.concatenate([s_sel.reshape(B, H, Q_BLOCK, n_sel), s_own], -1)
        p = jax.nn.softmax(logits, -1)
        p_sel = p[..., :n_sel].reshape(B, H, Q_BLOCK, topk, MOBA_BLOCK).astype(v.dtype)
        p_ow = p[..., n_sel:].astype(v.dtype)
        o = (jnp.einsum('bhqnk,bhqnkd->bhqd', p_sel, v_sel)
             + jnp.einsum('bhqk,bkhd->bhqd', p_ow, v_own))
        return o.transpose(0, 2, 1, 3)

    out = lax.map(chunk, jnp.arange(S // Q_BLOCK))
    return out.transpose(1, 0, 2, 3, 4).reshape(B, S, H * dh)


def sliding_window_attention(q, k, v, pos, tab, sinks):
    B, S, Hq, dh = q.shape
    G = k.shape[2]
    R = Hq // G
    nb = S // Q_BLOCK

    def band(t):
        tb = t.reshape((B, nb, Q_BLOCK) + t.shape[2:])
        prev = jnp.concatenate([jnp.zeros_like(tb[:, :1]), tb[:, :-1]], axis=1)
        return jnp.concatenate([prev, tb], axis=2)

    kb, vb, pk = band(k), band(v), band(pos)
    qb = q.reshape(B, nb, Q_BLOCK, G, R, dh)
    pq = pos.reshape(B, nb, Q_BLOCK)
    s = jnp.einsum('bnqgrd,bnkgd->bngrqk', qb, kb).astype(F32) * dh ** -0.5
    bias = t5_bias(tab, t5_bucket(pq[..., :, None] - pk[..., None, :])[:, None])
    bias = bias.reshape(B, G, R, nb, Q_BLOCK, 2 * Q_BLOCK).transpose(0, 3, 1, 2, 4, 5)
    qq = jnp.arange(Q_BLOCK)[:, None]
    kk = jnp.arange(2 * Q_BLOCK)[None, :]
    dist = Q_BLOCK + qq - kk
    in_win = (dist >= 0) & (dist < SWA_WINDOW)
    not_pad = (jnp.arange(nb)[:, None, None] * Q_BLOCK - Q_BLOCK + kk[None]) >= 0
    mask = in_win[None] & not_pad
    s = jnp.where(mask[None, :, None, None], s + bias, -jnp.inf)
    sink = jnp.broadcast_to(sinks.astype(F32).reshape(1, 1, G, R, 1, 1), s.shape[:-1] + (1,))
    p = jax.nn.softmax(jnp.concatenate([s, sink], -1), -1)[..., :-1]
    o = jnp.einsum('bngrqk,bnkgd->bnqgrd', p.astype(v.dtype), vb)
    return o.reshape(B, S, Hq * dh)


def diff_attention(q, k, v, pos, tab, lam_q1, lam_k1, lam_q2, lam_k2, subln_g, lambda_init):
    B, S, H, _, dh = q.shape
    lam = (jnp.exp(jnp.sum(lam_q1.astype(F32) * lam_k1.astype(F32)))
           - jnp.exp(jnp.sum(lam_q2.astype(F32) * lam_k2.astype(F32))) + lambda_init)
    scale = dh ** -0.5
    kt = k.transpose(0, 2, 3, 1, 4)
    vt = v.transpose(0, 2, 1, 3)
    ki = jnp.arange(S)

    def block(c):
        q0 = c * Q_BLOCK
        qc = lax.dynamic_slice_in_dim(q, q0, Q_BLOCK, axis=1)
        pq = lax.dynamic_slice_in_dim(pos, q0, Q_BLOCK, axis=1)
        s = jnp.einsum('bqhmd,bhmkd->bhmqk', qc, kt).astype(F32) * scale
        bias = t5_bias(tab, t5_bucket(pq[:, None, :, None] - pos[:, None, None, :]))
        causal = ki[None, :] <= (q0 + jnp.arange(Q_BLOCK))[:, None]
        a = jax.nn.softmax(jnp.where(causal, s + bias[:, :, None], -jnp.inf), -1)
        w = a[:, :, 0] - lam * a[:, :, 1]
        o = jnp.einsum('bhqk,bhkd->bqhd', w.astype(v.dtype), vt)
        return rms_norm(o, subln_g) * (1.0 - lambda_init)

    out = lax.map(block, jnp.arange(S // Q_BLOCK))
    return out.transpose(1, 0, 2, 3, 4).reshape(B, S, H * 2 * dh)


def mla_attention(c_q, c_kv, k_rope, pos, q_norm_g, kv_norm_g, w_uq, w_ukv):
    B, S, _ = c_q.shape
    qh = (rms_norm(c_q, q_norm_g) @ w_uq).reshape(B, S, D_HEADS, D_NOPE + D_ROPE)
    q_nope, q_pe = qh[..., :D_NOPE], rope(qh[..., D_NOPE:], pos)
    kv = (rms_norm(c_kv, kv_norm_g) @ w_ukv).reshape(B, S, D_HEADS, D_NOPE + D_V)
    k_nope, v = kv[..., :D_NOPE], kv[..., D_NOPE:]
    k_pe = rope(k_rope, pos)
    scale = (D_NOPE + D_ROPE) ** -0.5
    ki = jnp.arange(S)

    def block(c):
        q0 = c * Q_BLOCK
        qn = lax.dynamic_slice_in_dim(q_nope, q0, Q_BLOCK, axis=1)
        qp = lax.dynamic_slice_in_dim(q_pe, q0, Q_BLOCK, axis=1)
        s = (jnp.einsum('bqhd,bkhd->bhqk', qn, k_nope).astype(F32)
             + jnp.einsum('bqhd,bkd->bhqk', qp, k_pe).astype(F32)) * scale
        causal = ki[None, :] <= (q0 + jnp.arange(Q_BLOCK))[:, None]
        p = jax.nn.softmax(jnp.where(causal, s, -jnp.inf), -1)
        return jnp.einsum('bhqk,bkhd->bqhd', p.astype(v.dtype), v)

    out = lax.map(block, jnp.arange(S // Q_BLOCK))
    return out.transpose(1, 0, 2, 3, 4).reshape(B, S, D_HEADS * D_V)


def peer_ffn(x, w_pq, sub_keys, u_tab, v_tab):
    B, S, D = x.shape
    H, K = PEER_HEADS, PEER_TOPK
    q = (x @ w_pq).reshape(B, S, H, 2, PEER_D_KEY)
    s = jnp.einsum('bshmd,hmnd->bshmn', q, sub_keys).astype(F32)
    v_top, i_top = lax.top_k(s, K)
    cand = (v_top[..., 0, :, None] + v_top[..., 1, None, :]).reshape(B, S, H, K * K)
    cand_idx = (i_top[..., 0, :, None] * PEER_N_KEYS + i_top[..., 1, None, :]).reshape(B, S, H, K * K)
    best, pick = lax.top_k(cand, K)
    experts = jnp.take_along_axis(cand_idx, pick, -1)
    gate = jax.nn.softmax(best, -1)
    T = B * S
    nc = T // PEER_TOKEN_CHUNK

    def chunk(args):
        xc, ec, gc = args
        u = u_tab[ec]
        h = jax.nn.gelu(jnp.einsum('cd,chkd->chk', xc, u).astype(F32), approximate=False)
        return jnp.einsum('chk,chkd->cd', (gc * h).astype(x.dtype), v_tab[ec])

    out = lax.map(chunk, (x.reshape(nc, PEER_TOKEN_CHUNK, D),
                          experts.reshape(nc, PEER_TOKEN_CHUNK, H, K),
                          gate.reshape(nc, PEER_TOKEN_CHUNK, H, K)))
    return out.reshape(B, S, D)


def setup_inputs(seed: int = 0) -> dict:
    key = jax.random.key(seed)
    ks = jax.random.split(key, 26)
    L = DEPTH

    def nrm(k, shape, scale):
        return jax.random.normal(k, shape, F32) * scale

    def gain(k, shape):
        return 1.0 + 0.02 * jax.random.normal(k, shape, F32)

    return {
        'x': nrm(ks[0], (BATCH, SEQ, D_MODEL), 1.0),
        'positions': jnp.broadcast_to(jnp.arange(SEQ, dtype=jnp.int32), (BATCH, SEQ)),
        't5_table': nrm(ks[1], (T5_BUCKETS, BIAS_HEADS), 0.5),
        'w_in': nrm(ks[2], (L, D_MODEL, IN_W), D_MODEL ** -0.5),
        'w_gate': nrm(ks[3], (L, N_BRANCHES, D_MODEL, D_MODEL), D_MODEL ** -0.5),
        'b_gate': nrm(ks[4], (L, N_BRANCHES, D_MODEL), 0.1),
        'w_branch': nrm(ks[5], (L, N_BRANCHES, BRANCH_W, D_MODEL), DEEPNORM_BETA * BRANCH_W ** -0.5),
        'w_out': nrm(ks[6], (L, D_MODEL, D_MODEL), DEEPNORM_BETA * D_MODEL ** -0.5),
        'swa_sinks': nrm(ks[7], (L, B_HEADS), 0.5),
        'diff_lambda_q1': nrm(ks[8], (L, C_HEAD_DIM), 0.1),
        'diff_lambda_k1': nrm(ks[9], (L, C_HEAD_DIM), 0.1),
        'diff_lambda_q2': nrm(ks[10], (L, C_HEAD_DIM), 0.1),
        'diff_lambda_k2': nrm(ks[11], (L, C_HEAD_DIM), 0.1),
        'diff_subln_g': gain(ks[12], (L, 2 * C_HEAD_DIM)),
        'mla_q_norm_g': gain(ks[13], (L, D_Q_LORA)),
        'mla_kv_norm_g': gain(ks[14], (L, D_KV_LORA)),
        'mla_w_uq': nrm(ks[15], (L, D_Q_LORA, D_HEADS * (D_NOPE + D_ROPE)), D_Q_LORA ** -0.5),
        'mla_w_ukv': nrm(ks[16], (L, D_KV_LORA, D_HEADS * (D_NOPE + D_V)), D_KV_LORA ** -0.5),
        'ln1_g': gain(ks[17], (L, D_MODEL)),
        'ln1_b': nrm(ks[18], (L, D_MODEL), 0.02),
        'peer_w_q': nrm(ks[19], (L, D_MODEL, PEER_HEADS * 2 * PEER_D_KEY), D_MODEL ** -0.5),
        'peer_sub_keys': nrm(ks[20], (L, PEER_HEADS, 2, PEER_N_KEYS, PEER_D_KEY), PEER_D_KEY ** -0.5),
        'peer_u': nrm(ks[21], (L, PEER_N_EXPERTS, D_MODEL), D_MODEL ** -0.5),
        'peer_v': nrm(ks[22], (L, PEER_N_EXPERTS, D_MODEL), DEEPNORM_BETA),
        'ln2_g': gain(ks[23], (L, D_MODEL)),
        'ln2_b': nrm(ks[24], (L, D_MODEL), 0.02),
    }


def reference(x, positions, t5_table, w_in, w_gate, b_gate, w_branch, w_out, swa_sinks,
              diff_lambda_q1, diff_lambda_k1, diff_lambda_q2, diff_lambda_k2, diff_subln_g,
              mla_q_norm_g, mla_kv_norm_g, mla_w_uq, mla_w_ukv, ln1_g, ln1_b,
              peer_w_q, peer_sub_keys, peer_u, peer_v, ln2_g, ln2_b):
    B, S, _ = x.shape
    tab_a = t5_table[:, :A_HEADS]
    tab_b = t5_table[:, A_HEADS:A_HEADS + B_HEADS]
    tab_c = t5_table[:, A_HEADS + B_HEADS:]
    for layer in range(DEPTH):
        lambda_init = 0.8 - 0.6 * math.exp(-0.3 * layer)
        h = x @ w_in[layer]
        (a_q, a_k, a_v, b_q, b_k, b_v, c_q, c_k, c_v,
         d_cq, d_ckv, d_kr) = jnp.split(h, IN_SPLITS, axis=-1)
        o_a = moba_attention(a_q.reshape(B, S, A_HEADS, A_HEAD_DIM),
                             a_k.reshape(B, S, A_HEADS, A_HEAD_DIM),
                             a_v.reshape(B, S, A_HEADS, A_HEAD_DIM), positions, tab_a)
        o_b = sliding_window_attention(b_q.reshape(B, S, B_HEADS, B_HEAD_DIM),
                                       b_k.reshape(B, S, B_KV_HEADS, B_HEAD_DIM),
                                       b_v.reshape(B, S, B_KV_HEADS, B_HEAD_DIM),
                                       positions, tab_b, swa_sinks[layer])
        o_c = diff_attention(c_q.reshape(B, S, C_HEADS, 2, C_HEAD_DIM),
                             c_k.reshape(B, S, C_HEADS, 2, C_HEAD_DIM),
                             c_v.reshape(B, S, C_HEADS, 2 * C_HEAD_DIM), positions, tab_c,
                             diff_lambda_q1[layer], diff_lambda_k1[layer],
                             diff_lambda_q2[layer], diff_lambda_k2[layer],
                             diff_subln_g[layer], lambda_init)
        o_d = mla_attention(d_cq, d_ckv, d_kr, positions, mla_q_norm_g[layer], mla_kv_norm_g[layer],
                            mla_w_uq[layer], mla_w_ukv[layer])
        branches = (o_a, o_b, o_c, o_d)
        merged = sum(jax.nn.sigmoid(x @ w_gate[layer, n] + b_gate[layer, n]) * (branches[n] @ w_branch[layer, n])
                     for n in range(N_BRANCHES))
        x = layer_norm(DEEPNORM_ALPHA * x + merged @ w_out[layer], ln1_g[layer], ln1_b[layer])
        y = peer_ffn(x, peer_w_q[layer], peer_sub_keys[layer], peer_u[layer], peer_v[layer])
        x = layer_norm(DEEPNORM_ALPHA * x + y, ln2_g[layer], ln2_b[layer])
    return x
```

```python
import functools
import math

import jax
import jax.numpy as jnp
from jax import lax
from jax.experimental import pallas as pl
from jax.experimental.pallas import tpu as pltpu

F32 = jnp.float32
BF16 = jnp.bfloat16
LANE = 128
NEG = -1e30

D_MODEL = 2048
A_HEADS = 4
MOBA_BLOCK = 256
MOBA_TOPK = 3
B_HEADS = 8
B_KV_HEADS = 2
B_HEAD_DIM = 64
SWA_WINDOW = 128
C_HEADS = 4
C_HEAD_DIM = 64
D_HEADS = 4
D_Q_LORA = 512
D_KV_LORA = 256
D_NOPE = 128
D_ROPE = 64
D_V = 128
ROPE_THETA = 10000.0
N_BRANCHES = 4
BRANCH_W = 512
T5_BUCKETS = 32
T5_MAX_EXACT = T5_BUCKETS // 2
T5_MAX_DISTANCE = 1024
PEER_HEADS = 8
PEER_N_KEYS = 128
PEER_D_KEY = 128
PEER_TOPK = 16
LN_EPS = 1e-5
RMS_EPS = 1e-6

VMEM_LIMIT_MB = 56


def _cparams(*sem):
    return pltpu.CompilerParams(dimension_semantics=sem,
                                vmem_limit_bytes=VMEM_LIMIT_MB * 1024 * 1024)


_NT = (((1,), (1,)), ((), ()))
_TN = (((0,), (0,)), ((), ()))


def _proj_kernel(x_ref, w_ref, s_ref, o_ref):
    acc = jnp.dot(x_ref[...], w_ref[...], preferred_element_type=F32) * s_ref[...]
    for c in range(o_ref.shape[0]):
        o_ref[c] = acc[:, c * LANE:(c + 1) * LANE].astype(o_ref.dtype)


def proj_slots(x_bf, w_bf, colscale, out_dtype, *, tm, slots_per_step):
    S, K = x_bf.shape
    N = w_bf.shape[1]
    tn = slots_per_step * LANE
    return pl.pallas_call(
        _proj_kernel,
        grid=(N // tn, S // tm),
        in_specs=[pl.BlockSpec((tm, K), lambda j, i: (i, 0)),
                  pl.BlockSpec((K, tn), lambda j, i: (0, j)),
                  pl.BlockSpec((1, tn), lambda j, i: (0, j))],
        out_specs=pl.BlockSpec((slots_per_step, tm, LANE), lambda j, i: (j, i, 0)),
        out_shape=jax.ShapeDtypeStruct((N // LANE, S, LANE), out_dtype),
        compiler_params=_cparams("parallel", "parallel"),
        name="proj_slots",
    )(x_bf, w_bf, colscale)


def _t5_tiles_kernel(tab_ref, o_ref, *, T, n_heads):
    dj = pl.program_id(0)
    row = lax.broadcasted_iota(jnp.int32, (T, T), 0)
    col = lax.broadcasted_iota(jnp.int32, (T, T), 1)
    n = jnp.maximum(row - col + dj * T, 0)
    nf = jnp.maximum(n, 1).astype(F32)
    large = T5_MAX_EXACT + (jnp.log(nf / T5_MAX_EXACT) / math.log(T5_MAX_DISTANCE / T5_MAX_EXACT)
                            * (T5_BUCKETS - T5_MAX_EXACT)).astype(jnp.int32)
    large = jnp.minimum(large, T5_BUCKETS - 1)
    bucket = jnp.where(n < T5_MAX_EXACT, n, large)
    for h in range(n_heads):
        last = tab_ref[T5_BUCKETS - 1, h]
        acc = jnp.zeros((T, T), F32)
        for b in range(T5_BUCKETS - 1):
            acc = jnp.where(bucket == b, tab_ref[b, h] - last, acc)
        o_ref[h, 0] = acc


def t5_tiles(tab, *, T, n_near):
    H = tab.shape[1]
    return pl.pallas_call(
        functools.partial(_t5_tiles_kernel, T=T, n_heads=H),
        grid=(n_near,),
        in_specs=[pl.BlockSpec(memory_space=pltpu.SMEM)],
        out_specs=pl.BlockSpec((H, 1, T, T), lambda d: (0, d, 0, 0)),
        out_shape=jax.ShapeDtypeStruct((H, n_near, T, T), F32),
        compiler_params=_cparams("parallel"),
        name="t5_tiles",
    )(tab)


def _n_near(T):
    return (T5_MAX_DISTANCE - 1 + T - 1) // T + 1


def _kmean_kernel(k_ref, o_ref, *, blk):
    k = k_ref[...].astype(F32)
    nb = k.shape[0] // blk
    o_ref[...] = k.reshape(nb, blk, LANE).sum(axis=1) * (1.0 / blk)


def moba_kmean(hs, k_slot0, S):
    nb = S // MOBA_BLOCK
    return pl.pallas_call(
        functools.partial(_kmean_kernel, blk=MOBA_BLOCK),
        grid=(A_HEADS,),
        in_specs=[pl.BlockSpec((None, S, LANE), lambda h: (k_slot0 + h, 0, 0))],
        out_specs=pl.BlockSpec((None, nb, LANE), lambda h: (h, 0, 0)),
        out_shape=jax.ShapeDtypeStruct((A_HEADS, nb, LANE), F32),
        compiler_params=_cparams("parallel"),
        name="moba_kmean",
    )(hs)


def _moba_select_kernel(q_ref, km_ref, o_ref, *, topk):
    own = pl.program_id(1)
    q = q_ref[...].astype(F32)
    gate = lax.dot_general(q, km_ref[...], _NT, precision=lax.Precision.HIGHEST,
                           preferred_element_type=F32)
    lane = lax.broadcasted_iota(jnp.int32, gate.shape, 1)
    nbp = gate.shape[1]
    gate = jnp.where(lane < own, gate, -jnp.inf)
    mask = jnp.where(lane == own, 0.0, NEG)
    for _ in range(topk):
        mx = jnp.max(gate, axis=-1, keepdims=True)
        idx = jnp.min(jnp.where(gate == mx, lane, nbp), axis=-1, keepdims=True)
        hit = lane == idx
        mask = jnp.where(hit, jnp.where(mx > -jnp.inf, 0.0, mask), mask)
        gate = jnp.where(hit, -jnp.inf, gate)
    o_ref[...] = mask.astype(o_ref.dtype)


def moba_select(hs, q_slot0, kmean_pad, S):
    nbp = kmean_pad.shape[1]
    return pl.pallas_call(
        functools.partial(_moba_select_kernel, topk=MOBA_TOPK),
        grid=(A_HEADS, S // MOBA_BLOCK),
        in_specs=[pl.BlockSpec((None, MOBA_BLOCK, LANE), lambda h, i: (q_slot0 + h, i, 0)),
                  pl.BlockSpec((None, nbp, LANE), lambda h, i: (h, 0, 0))],
        out_specs=pl.BlockSpec((None, MOBA_BLOCK, nbp), lambda h, i: (h, i, 0)),
        out_shape=jax.ShapeDtypeStruct((A_HEADS, S, nbp), BF16),
        compiler_params=_cparams("parallel", "parallel"),
        name="moba_select",
    )(hs, kmean_pad)


def _flash_kernel(*refs, mode, T, n_near, has_far, window, lam_init):
    it = iter(refs)
    q_ref, k_ref, v_ref = next(it), next(it), next(it)
    bias_ref = next(it) if mode in "ABC" else None
    sink_ref = next(it) if mode == "B" else None
    selb_ref = next(it) if mode == "A" else None
    lam_ref, g_ref = (next(it), next(it)) if mode == "C" else (None, None)
    o_ref = next(it)

    u = pl.program_id(0)
    qi = pl.program_id(1)
    q = q_ref[...]
    half = LANE // 2
    if mode == "A":
        qs = [jnp.concatenate([q, selb_ref[...]], axis=1)]
    elif mode == "B":
        z = jnp.zeros((T, half), q.dtype)
        qs = [jnp.concatenate([q[:, :half], z], axis=1), jnp.concatenate([q[:, half:], z], axis=1)]
    elif mode == "C":
        lane = lax.broadcasted_iota(jnp.int32, q.shape, 1)
        zq = jnp.zeros_like(q)
        qs = [jnp.where(lane < half, q, zq), jnp.where(lane >= half, q, zq)]
    else:
        qs = [q]
    n_sub = len(qs)
    dv = v_ref.shape[-1]

    def tile(j, carry, near):
        start = pl.multiple_of(j * T, T)
        kt = k_ref[pl.ds(start, T), :]
        vt = v_ref[pl.ds(start, T), :]
        if mode == "A":
            blk = lax.broadcasted_iota(jnp.int32, (T, selb_ref.shape[-1]), 1)
            onehot = jnp.where(blk == j, 1.0, 0.0).astype(kt.dtype)
            kt = jnp.concatenate([kt, onehot], axis=1)
        if near:
            dj = qi - j
            row = lax.broadcasted_iota(jnp.int32, (T, T), 0)
            col = lax.broadcasted_iota(jnp.int32, (T, T), 1)
            dist = row - col + dj * T
            valid = dist >= 0
            if window is not None:
                valid = valid & (dist < window)
        out = []
        for c in range(n_sub):
            m, l, acc = carry[c]
            s = lax.dot_general(qs[c], kt, _NT, preferred_element_type=F32)
            if near:
                if bias_ref is not None:
                    s = s + bias_ref[c if bias_ref.shape[0] > 1 else 0, dj]
                s = jnp.where(valid, s, NEG)
            m_new = jnp.maximum(m, jnp.max(s, axis=-1, keepdims=True))
            alpha = jnp.exp(m - m_new)
            p = jnp.exp(s - m_new)
            l = alpha * l + jnp.sum(p, axis=-1, keepdims=True)
            acc = alpha * acc + jnp.dot(p.astype(vt.dtype), vt, preferred_element_type=F32)
            out.append((m_new, l, acc))
        return tuple(out)

    init = []
    for c in range(n_sub):
        if sink_ref is not None:
            m0 = jnp.full((T, 1), sink_ref[u, c], F32)
            l0 = jnp.ones((T, 1), F32)
        else:
            m0 = jnp.full((T, 1), NEG, F32)
            l0 = jnp.zeros((T, 1), F32)
        init.append((m0, l0, jnp.zeros((T, dv), F32)))
    carry = tuple(init)

    near_lo = jnp.maximum(qi - n_near + 1, 0)
    if has_far:
        carry = lax.fori_loop(0, near_lo, lambda j, c: tile(j, c, False), carry)
    carry = lax.fori_loop(near_lo, qi + 1, lambda j, c: tile(j, c, True), carry)

    outs = [acc / l for (_, l, acc) in carry]
    if mode == "B":
        o = jnp.concatenate([outs[0][:, :half], outs[1][:, :half]], axis=1)
    elif mode == "C":
        lp = lam_ref[...]
        lam = (jnp.exp(jnp.sum(lp[0:1] * lp[1:2], axis=-1, keepdims=True))
               - jnp.exp(jnp.sum(lp[2:3] * lp[3:4], axis=-1, keepdims=True)) + lam_init)
        o = outs[0] - lam * outs[1]
        o = o * lax.rsqrt(jnp.mean(o * o, axis=-1, keepdims=True) + RMS_EPS) * g_ref[...]
        o = o * (1.0 - lam_init)
    else:
        o = outs[0]
    o_ref[...] = o.astype(o_ref.dtype)


def flash(mode, q_arr, k_arr, v_arr, *, S, T, q_slot, k_slot, v_slot, n_units=4, bias=None,
          sinks=None, selb=None, lam_params=None, subln_g=None, lam_init=0.0):
    dq = q_arr.shape[-1]
    dk = k_arr.shape[-1]
    dv = v_arr.shape[-1]
    if mode == "B":
        n_near, has_far, window = 2, False, SWA_WINDOW
    elif mode == "D":
        n_near, has_far, window = 1, True, None
    else:
        n_near, has_far, window = _n_near(T), True, None
    in_specs = [pl.BlockSpec((None, T, dq), lambda u, i: (q_slot(u), i, 0)),
                pl.BlockSpec((None, S, dk), lambda u, i: (k_slot(u), 0, 0)),
                pl.BlockSpec((None, S, dv), lambda u, i: (v_slot(u), 0, 0))]
    args = [q_arr, k_arr, v_arr]
    if bias is not None:
        in_specs.append(pl.BlockSpec((None,) + bias.shape[1:], lambda u, i: (u, 0, 0, 0, 0)))
        args.append(bias)
    if sinks is not None:
        in_specs.append(pl.BlockSpec(memory_space=pltpu.SMEM))
        args.append(sinks)
    if selb is not None:
        in_specs.append(pl.BlockSpec((None, T, selb.shape[-1]), lambda u, i: (u, i, 0)))
        args.append(selb)
    if lam_params is not None:
        in_specs.append(pl.BlockSpec(lam_params.shape, lambda u, i: (0, 0)))
        in_specs.append(pl.BlockSpec(subln_g.shape, lambda u, i: (0, 0)))
        args += [lam_params, subln_g]
    return pl.pallas_call(
        functools.partial(_flash_kernel, mode=mode, T=T, n_near=n_near, has_far=has_far,
                          window=window, lam_init=lam_init),
        grid=(n_units, S // T),
        in_specs=in_specs,
        out_specs=pl.BlockSpec((None, T, LANE), lambda u, i: (u, i, 0)),
        out_shape=jax.ShapeDtypeStruct((n_units, S, LANE), BF16),
        compiler_params=_cparams("parallel", "arbitrary"),
        name="flash_" + mode,
    )(*args)


def _rms(x, g):
    return x * lax.rsqrt(jnp.mean(x * x, axis=-1, keepdims=True) + RMS_EPS) * g


def _mla_prep_kernel(hd_ref, gq_ref, gkv_ref, wqa_ref, wqb_ref, wkv_ref, cos_ref, sin_ref,
                     qo_ref, ko_ref, vo_ref, *, scale):
    cosm = cos_ref[...]
    sinm = sin_ref[...]
    cq = jnp.concatenate([hd_ref[c] for c in range(4)], axis=1)
    cqn = _rms(cq, gq_ref[...]).astype(BF16)
    qa = jnp.dot(cqn, wqa_ref[...], preferred_element_type=F32)
    qb = jnp.dot(cqn, wqb_ref[...], preferred_element_type=F32)
    w = 2 * LANE
    for h in range(D_HEADS):
        qe = (qa[:, h * w:(h + 1) * w] * cosm + qb[:, h * w:(h + 1) * w] * sinm) * scale
        qo_ref[h] = qe.astype(qo_ref.dtype)
    ckv = jnp.concatenate([hd_ref[4], hd_ref[5]], axis=1)
    ckvn = _rms(ckv, gkv_ref[...]).astype(BF16)
    kv = jnp.dot(ckvn, wkv_ref[...], preferred_element_type=F32)
    kpe = hd_ref[6] * cosm[:, LANE:] + hd_ref[7] * sinm[:, LANE:]
    for h in range(D_HEADS):
        ko_ref[h] = jnp.concatenate([kv[:, h * LANE:(h + 1) * LANE], kpe], axis=1).astype(ko_ref.dtype)
        vo_ref[h] = kv[:, (D_HEADS + h) * LANE:(D_HEADS + h + 1) * LANE].astype(vo_ref.dtype)


def mla_prep(hd, gq, gkv, wqa, wqb, wkv, cosm, sinm, *, S, tm):
    w = 2 * LANE
    full = lambda a: pl.BlockSpec(a.shape, lambda i: (0,) * a.ndim)
    return pl.pallas_call(
        functools.partial(_mla_prep_kernel, scale=(D_NOPE + D_ROPE) ** -0.5),
        grid=(S // tm,),
        in_specs=[pl.BlockSpec((8, tm, LANE), lambda i: (0, i, 0)),
                  full(gq), full(gkv), full(wqa), full(wqb), full(wkv),
                  pl.BlockSpec((tm, w), lambda i: (i, 0)),
                  pl.BlockSpec((tm, w), lambda i: (i, 0))],
        out_specs=[pl.BlockSpec((D_HEADS, tm, w), lambda i: (0, i, 0)),
                   pl.BlockSpec((D_HEADS, tm, w), lambda i: (0, i, 0)),
                   pl.BlockSpec((D_HEADS, tm, LANE), lambda i: (0, i, 0))],
        out_shape=[jax.ShapeDtypeStruct((D_HEADS, S, w), BF16),
                   jax.ShapeDtypeStruct((D_HEADS, S, w), BF16),
                   jax.ShapeDtypeStruct((D_HEADS, S, LANE), BF16)],
        compiler_params=_cparams("parallel"),
        name="mla_prep",
    )(hd, gq, gkv, wqa, wqb, wkv, cosm, sinm)


def _merge_kernel(x_ref, oa_ref, ob_ref, oc_ref, od_ref, wg_ref, bg_ref, wbr_ref, o_ref):
    x = x_ref[...]
    acc = None
    for n, o in enumerate((oa_ref, ob_ref, oc_ref, od_ref)):
        g = jnp.dot(x, wg_ref[n], preferred_element_type=F32) + bg_ref[n]
        ocat = jnp.concatenate([o[c] for c in range(4)], axis=1)
        b = jnp.dot(ocat, wbr_ref[n], preferred_element_type=F32)
        t = jax.nn.sigmoid(g) * b
        acc = t if acc is None else acc + t
    o_ref[...] = acc.astype(o_ref.dtype)


def merge(x_bf, outs, wg, bg, wbr, *, tm, tn):
    S, D = x_bf.shape
    o_spec = pl.BlockSpec((4, tm, LANE), lambda j, i: (0, i, 0))
    return pl.pallas_call(
        _merge_kernel,
        grid=(D // tn, S // tm),
        in_specs=[pl.BlockSpec((tm, D), lambda j, i: (i, 0)), o_spec, o_spec, o_spec, o_spec,
                  pl.BlockSpec((N_BRANCHES, D, tn), lambda j, i: (0, 0, j)),
                  pl.BlockSpec((N_BRANCHES, 1, tn), lambda j, i: (0, 0, j)),
                  pl.BlockSpec((N_BRANCHES, BRANCH_W, tn), lambda j, i: (0, 0, j))],
        out_specs=pl.BlockSpec((tm, tn), lambda j, i: (i, j)),
        out_shape=jax.ShapeDtypeStruct((S, D), BF16),
        compiler_params=_cparams("parallel", "parallel"),
        name="merge",
    )(x_bf, *outs, wg, bg, wbr)


def _layer_norm(y, g, b):
    mu = jnp.mean(y, axis=-1, keepdims=True)
    var = jnp.mean(jnp.square(y - mu), axis=-1, keepdims=True)
    return (y - mu) * lax.rsqrt(var + LN_EPS) * g + b


def _out_ln_kernel(x_ref, m_ref, w_ref, g_ref, b_ref, of_ref, ob_ref, *, alpha):
    y = alpha * x_ref[...] + jnp.dot(m_ref[...], w_ref[...], preferred_element_type=F32)
    o = _layer_norm(y, g_ref[...], b_ref[...])
    of_ref[...] = o
    ob_ref[...] = o.astype(ob_ref.dtype)


def out_ln(x, merged, w_out, g, b, *, alpha, tm):
    S, D = x.shape
    row = pl.BlockSpec((tm, D), lambda i: (i, 0))
    vec = pl.BlockSpec((1, D), lambda i: (0, 0))
    return pl.pallas_call(
        functools.partial(_out_ln_kernel, alpha=alpha),
        grid=(S // tm,),
        in_specs=[row, row, pl.BlockSpec((D, D), lambda i: (0, 0)), vec, vec],
        out_specs=[row, row],
        out_shape=[jax.ShapeDtypeStruct((S, D), F32), jax.ShapeDtypeStruct((S, D), BF16)],
        compiler_params=_cparams("parallel"),
        name="out_ln",
    )(x, merged, w_out, g, b)


def _add_ln_kernel(x_ref, y_ref, g_ref, b_ref, of_ref, ob_ref, *, alpha):
    o = _layer_norm(alpha * x_ref[...] + y_ref[...], g_ref[...], b_ref[...])
    of_ref[...] = o
    ob_ref[...] = o.astype(ob_ref.dtype)


def add_ln(x, y, g, b, *, alpha, tm):
    S, D = x.shape
    row = pl.BlockSpec((tm, D), lambda i: (i, 0))
    vec = pl.BlockSpec((1, D), lambda i: (0, 0))
    return pl.pallas_call(
        functools.partial(_add_ln_kernel, alpha=alpha),
        grid=(S // tm,),
        in_specs=[row, row, vec, vec],
        out_specs=[row, row],
        out_shape=[jax.ShapeDtypeStruct((S, D), F32), jax.ShapeDtypeStruct((S, D), BF16)],
        compiler_params=_cparams("parallel"),
        name="add_ln",
    )(x, y, g, b)


def _top_sorted(s, k):
    vals = []
    cur = s
    for r in range(k):
        m = jnp.max(cur, axis=0, keepdims=True)
        vals.append(m)
        if r + 1 < k:
            cur = jnp.where(cur == m, -jnp.inf, cur)
    return vals


def _peer_route_kernel(q_ref, keys_ref, s0_ref, s1_ref, e1_ref, c_ref, tau_ref):
    K = PEER_TOPK

    def head(h, _):
        s0 = lax.dot_general(keys_ref[h, 0], q_ref[2 * h], _NT, precision=lax.Precision.HIGHEST,
                             preferred_element_type=F32)
        s1 = lax.dot_general(keys_ref[h, 1], q_ref[2 * h + 1], _NT, precision=lax.Precision.HIGHEST,
                             preferred_element_type=F32)
        v0 = _top_sorted(s0, K)
        v1 = jnp.concatenate(_top_sorted(s1, K), axis=0)
        cands = [v0[a] + v1 for a in range(K)]
        z = None
        for r in range(K):
            m = functools.reduce(jnp.maximum, cands)
            m = jnp.max(m, axis=0, keepdims=True)
            if r == 0:
                top = m
                z = jnp.ones_like(m)
            else:
                z = z + jnp.exp(m - top)
            if r + 1 < K:
                cands = [jnp.where(c == m, -jnp.inf, c) for c in cands]
        s0_ref[h] = s0
        s1_ref[h] = s1
        e1_ref[h] = jnp.exp(s1 - v1[0:1])
        c_ref[h] = jnp.exp(s0 - v0[0]) / z
        tau_ref[pl.ds(h, 1), :] = m
        return 0

    lax.fori_loop(0, PEER_HEADS, head, 0)


def peer_route(q_slots, sub_keys, *, S, tq):
    arr = jax.ShapeDtypeStruct((PEER_HEADS, PEER_N_KEYS, S), F32)
    spec = pl.BlockSpec((PEER_HEADS, PEER_N_KEYS, tq), lambda i: (0, 0, i))
    return pl.pallas_call(
        _peer_route_kernel,
        grid=(S // tq,),
        in_specs=[pl.BlockSpec((2 * PEER_HEADS, tq, LANE), lambda i: (0, i, 0)),
                  pl.BlockSpec(sub_keys.shape, lambda i: (0, 0, 0, 0))],
        out_specs=[spec, spec, spec, spec, pl.BlockSpec((PEER_HEADS, tq), lambda i: (0, i))],
        out_shape=[arr, arr, arr, arr, jax.ShapeDtypeStruct((PEER_HEADS, S), F32)],
        compiler_params=_cparams("parallel"),
        name="peer_route",
    )(q_slots, sub_keys)


def _peer_expert_kernel(x_ref, u_ref, v_ref, s0_ref, s1_ref, e1_ref, c_ref, tau_ref, y_ref):
    e = pl.program_id(1)
    n_rows = u_ref.shape[0] // PEER_N_KEYS

    @pl.when(e == 0)
    def _():
        y_ref[...] = jnp.zeros_like(y_ref)

    ht = lax.dot_general(u_ref[...], x_ref[...], _NT, preferred_element_type=F32)
    act = 0.5 * ht * (1.0 + lax.erf(ht * math.sqrt(0.5)))
    ws = []
    for r in range(n_rows):
        i = e * n_rows + r
        g = None
        for h in range(PEER_HEADS):
            ssum = s0_ref[h, pl.ds(i, 1), :] + s1_ref[h]
            t = jnp.where(ssum >= tau_ref[pl.ds(h, 1), :], e1_ref[h], 0.0) * c_ref[h, pl.ds(i, 1), :]
            g = t if g is None else g + t
        ws.append((g * act[r * PEER_N_KEYS:(r + 1) * PEER_N_KEYS, :]).astype(BF16))
    wt = jnp.concatenate(ws, axis=0) if n_rows > 1 else ws[0]
    y_ref[...] += lax.dot_general(wt, v_ref[...], _TN, preferred_element_type=F32)


def peer_experts(x_bf, u_bf, v_bf, s0, s1, e1, c, tau, *, tq, te):
    S, D = x_bf.shape
    E = u_bf.shape[0]
    rt = pl.BlockSpec((PEER_HEADS, PEER_N_KEYS, tq), lambda i, e: (0, 0, i))
    return pl.pallas_call(
        _peer_expert_kernel,
        grid=(S // tq, E // te),
        in_specs=[pl.BlockSpec((tq, D), lambda i, e: (i, 0)),
                  pl.BlockSpec((te, D), lambda i, e: (e, 0)),
                  pl.BlockSpec((te, D), lambda i, e: (e, 0)),
                  rt, rt, rt, rt,
                  pl.BlockSpec((PEER_HEADS, tq), lambda i, e: (0, i))],
        out_specs=pl.BlockSpec((tq, D), lambda i, e: (i, 0)),
        out_shape=jax.ShapeDtypeStruct((S, D), F32),
        compiler_params=_cparams("parallel", "arbitrary"),
        name="peer_experts",
    )(x_bf, u_bf, v_bf, s0, s1, e1, c, tau)


def _pad_cols(w, width):
    return jnp.pad(w, ((0, 0), (0, width - w.shape[1])))


def _rot_cols(w):
    h = w.shape[1] // 2
    return jnp.concatenate([-w[:, h:], w[:, :h]], axis=1)


def _split_w_in(w):
    widths = (512, 512, 512, 512, 128, 128, 512, 512, 512, D_Q_LORA, D_KV_LORA, D_ROPE)
    out, o = [], 0
    for n in widths:
        out.append(w[:, o:o + n])
        o += n
    return out


def _layer_weights(w_in, a_scale, b_scale, c_scale):
    (a_q, a_k, a_v, b_q, b_k, b_v, c_q, c_k, c_v, d_cq, d_ckv, d_kr) = _split_w_in(w_in)
    b_k0, b_k1 = b_k[:, :64], b_k[:, 64:]
    b_v0, b_v1 = b_v[:, :64], b_v[:, 64:]
    w_abc = jnp.concatenate(
        [a_q, a_k, a_v,
         b_q, _pad_cols(b_k0, LANE), _pad_cols(b_k1, LANE), _pad_cols(b_v0, LANE), _pad_cols(b_v1, LANE),
         c_q, c_k, c_v], axis=1).astype(BF16)
    ones = lambda n: jnp.ones((n,), F32)
    s_abc = jnp.concatenate([ones(512) * a_scale, ones(1024),
                             ones(512) * b_scale, ones(512),
                             ones(512) * c_scale, ones(1024)])[None, :]
    w_d = jnp.concatenate([d_cq, d_ckv, _pad_cols(d_kr, LANE), _pad_cols(_rot_cols(d_kr), LANE)],
                          axis=1).astype(BF16)
    return w_abc, s_abc, w_d


def _mla_weights(w_uq, w_ukv):
    wq = w_uq.reshape(D_Q_LORA, D_HEADS, D_NOPE + D_ROPE)
    nope, pe = wq[..., :D_NOPE], wq[..., D_NOPE:]
    pe_rot = jnp.concatenate([-pe[..., D_ROPE // 2:], pe[..., :D_ROPE // 2]], axis=-1)
    z64 = jnp.zeros((D_Q_LORA, D_HEADS, LANE - D_ROPE), F32)
    z128 = jnp.zeros((D_Q_LORA, D_HEADS, D_NOPE), F32)
    wqa = jnp.concatenate([nope, pe, z64], axis=-1).reshape(D_Q_LORA, -1).astype(BF16)
    wqb = jnp.concatenate([z128, pe_rot, z64], axis=-1).reshape(D_Q_LORA, -1).astype(BF16)
    wkv = w_ukv.reshape(D_KV_LORA, D_HEADS, D_NOPE + D_V)
    wkv = jnp.concatenate([wkv[..., :D_NOPE].reshape(D_KV_LORA, -1),
                           wkv[..., D_NOPE:].reshape(D_KV_LORA, -1)], axis=1).astype(BF16)
    return wqa, wqb, wkv


def _rope_tables(positions):
    half = D_ROPE // 2
    inv = ROPE_THETA ** (-jnp.arange(half, dtype=F32) / half)
    ang = positions.astype(F32)[..., None] * inv
    cos, sin = jnp.cos(ang), jnp.sin(ang)
    S = positions.shape[0]
    one = jnp.ones((S, D_NOPE), F32)
    z = jnp.zeros((S, LANE - D_ROPE), F32)
    cosm = jnp.concatenate([one, cos, cos, z], axis=1)
    sinm = jnp.concatenate([jnp.zeros((S, D_NOPE), F32), sin, sin, z], axis=1)
    return cosm, sinm


def kernel(x, positions, t5_table, w_in, w_gate, b_gate, w_branch, w_out, swa_sinks, diff_lambda_q1, diff_lambda_k1, diff_lambda_q2, diff_lambda_k2, diff_subln_g, mla_q_norm_g, mla_kv_norm_g, mla_w_uq, mla_w_ukv, ln1_g, ln1_b, peer_w_q, peer_sub_keys, peer_u, peer_v, ln2_g, ln2_b):
    B, S, D = x.shape
    assert B == 1 and D == D_MODEL and S % MOBA_BLOCK == 0
    depth = w_in.shape[0]
    alpha = (2.0 * depth) ** 0.25
    tm = min(512, S)
    T = MOBA_BLOCK
    TB = SWA_WINDOW

    tab_a = t5_table[:, :A_HEADS]
    tab_b = t5_table[:, A_HEADS:A_HEADS + B_HEADS]
    tab_c = t5_table[:, A_HEADS + B_HEADS:]
    bias_a = t5_tiles(tab_a, T=T, n_near=_n_near(T))[:, None]
    bias_c = t5_tiles(tab_c, T=T, n_near=_n_near(T))[:, None]
    bias_b = t5_tiles(tab_b, T=TB, n_near=2).reshape(B_HEADS // 2, 2, 2, TB, TB)
    cosm, sinm = _rope_tables(positions[0])

    xf = x[0]
    xb = xf.astype(BF16)
    for layer in range(depth):
        lam_init = 0.8 - 0.6 * math.exp(-0.3 * layer)
        w_abc, s_abc, w_d = _layer_weights(w_in[layer], 128 ** -0.5, B_HEAD_DIM ** -0.5, C_HEAD_DIM ** -0.5)
        hs = proj_slots(xb, w_abc, s_abc, BF16, tm=tm, slots_per_step=8)
        hd = proj_slots(xb, w_d, jnp.ones((1, w_d.shape[1]), F32), F32, tm=tm, slots_per_step=8)

        nb = S // MOBA_BLOCK
        nbp = -(-nb // LANE) * LANE
        kmean = jnp.pad(moba_kmean(hs, 4, S), ((0, 0), (0, nbp - nb), (0, 0)))
        selb = moba_select(hs, 0, kmean, S)
        o_a = flash("A", hs, hs, hs, S=S, T=T, q_slot=lambda u: u, k_slot=lambda u: 4 + u,
                    v_slot=lambda u: 8 + u, bias=bias_a, selb=selb)
        sinks = (swa_sinks[layer] - tab_b[T5_BUCKETS - 1]).reshape(B_HEADS // 2, 2)
        o_b = flash("B", hs, hs, hs, S=S, T=TB, q_slot=lambda u: 12 + u, k_slot=lambda u: 16 + u // 2,
                    v_slot=lambda u: 18 + u // 2, bias=bias_b, sinks=sinks)
        lam_params = jnp.stack([diff_lambda_q1[layer], diff_lambda_k1[layer],
                                diff_lambda_q2[layer], diff_lambda_k2[layer]])
        o_c = flash("C", hs, hs, hs, S=S, T=T, q_slot=lambda u: 20 + u, k_slot=lambda u: 24 + u,
                    v_slot=lambda u: 28 + u, bias=bias_c, lam_params=lam_params,
                    subln_g=diff_subln_g[layer][None, :], lam_init=lam_init)
        wqa, wqb, wkv = _mla_weights(mla_w_uq[layer], mla_w_ukv[layer])
        q_d, k_d, v_d = mla_prep(hd, mla_q_norm_g[layer][None, :], mla_kv_norm_g[layer][None, :],
                                 wqa, wqb, wkv, cosm, sinm, S=S, tm=tm)
        o_d = flash("D", q_d, k_d, v_d, S=S, T=T, q_slot=lambda u: u, k_slot=lambda u: u,
                    v_slot=lambda u: u)

        merged = merge(xb, (o_a, o_b, o_c, o_d),
                       w_gate[layer].astype(BF16), b_gate[layer][:, None, :], w_branch[layer].astype(BF16), tm=tm, tn=512)
        xf, xb = out_ln(xf, merged, w_out[layer].astype(BF16), ln1_g[layer][None, :],
                        ln1_b[layer][None, :], alpha=alpha, tm=tm)

        q_slots = proj_slots(xb, peer_w_q[layer].astype(BF16), jnp.ones((1, D), F32), F32,
                             tm=tm, slots_per_step=8)
        s0, s1, e1, c, tau = peer_route(q_slots, peer_sub_keys[layer], S=S, tq=LANE)
        y = peer_experts(xb, peer_u[layer].astype(BF16), peer_v[layer].astype(BF16),
                         s0, s1, e1, c, tau, tq=tm, te=512)
        xf, xb = add_ln(xf, y, ln2_g[layer][None, :], ln2_b[layer][None, :], alpha=alpha, tm=tm)
    return xf[None]
```

```python
import functools
import math

import jax
import jax.numpy as jnp
from jax import lax
from jax.experimental import pallas as pl
from jax.experimental.pallas import tpu as pltpu

F32 = jnp.float32
BF16 = jnp.bfloat16
LANE = 128
NEG = -1e30

D_MODEL = 2048
A_HEADS = 4
MOBA_BLOCK = 256
MOBA_TOPK = 3
B_HEADS = 8
B_KV_HEADS = 2
B_HEAD_DIM = 64
SWA_WINDOW = 128
C_HEADS = 4
C_HEAD_DIM = 64
D_HEADS = 4
D_Q_LORA = 512
D_KV_LORA = 256
D_NOPE = 128
D_ROPE = 64
D_V = 128
ROPE_THETA = 10000.0
N_BRANCHES = 4
BRANCH_W = 512
T5_BUCKETS = 32
T5_MAX_EXACT = T5_BUCKETS // 2
T5_MAX_DISTANCE = 1024
PEER_HEADS = 8
PEER_N_KEYS = 128
PEER_D_KEY = 128
PEER_TOPK = 16
LN_EPS = 1e-5
RMS_EPS = 1e-6

VMEM_LIMIT_MB = 56


def _cparams(*sem):
    return pltpu.CompilerParams(dimension_semantics=sem,
                                vmem_limit_bytes=VMEM_LIMIT_MB * 1024 * 1024)


_NT = (((1,), (1,)), ((), ()))
_TN = (((0,), (0,)), ((), ()))


def _proj_kernel(x_ref, w_ref, s_ref, o_ref):
    acc = jnp.dot(x_ref[...], w_ref[...], preferred_element_type=F32) * s_ref[...]
    for c in range(o_ref.shape[0]):
        o_ref[c] = acc[:, c * LANE:(c + 1) * LANE].astype(o_ref.dtype)


def proj_slots(x_bf, w_bf, colscale, out_dtype, *, tm, slots_per_step):
    S, K = x_bf.shape
    N = w_bf.shape[1]
    tn = slots_per_step * LANE
    return pl.pallas_call(
        _proj_kernel,
        grid=(N // tn, S // tm),
        in_specs=[pl.BlockSpec((tm, K), lambda j, i: (i, 0)),
                  pl.BlockSpec((K, tn), lambda j, i: (0, j)),
                  pl.BlockSpec((1, tn), lambda j, i: (0, j))],
        out_specs=pl.BlockSpec((slots_per_step, tm, LANE), lambda j, i: (j, i, 0)),
        out_shape=jax.ShapeDtypeStruct((N // LANE, S, LANE), out_dtype),
        compiler_params=_cparams("parallel", "parallel"),
        name="proj_slots",
    )(x_bf, w_bf, colscale)


def _t5_tiles_kernel(tab_ref, o_ref, *, T, n_heads):
    dj = pl.program_id(0)
    row = lax.broadcasted_iota(jnp.int32, (T, T), 0)
    col = lax.broadcasted_iota(jnp.int32, (T, T), 1)
    n = jnp.maximum(row - col + dj * T, 0)
    nf = jnp.maximum(n, 1).astype(F32)
    large = T5_MAX_EXACT + (jnp.log(nf / T5_MAX_EXACT) / math.log(T5_MAX_DISTANCE / T5_MAX_EXACT)
                            * (T5_BUCKETS - T5_MAX_EXACT)).astype(jnp.int32)
    large = jnp.minimum(large, T5_BUCKETS - 1)
    bucket = jnp.where(n < T5_MAX_EXACT, n, large)
    for h in range(n_heads):
        last = tab_ref[T5_BUCKETS - 1, h]
        acc = jnp.zeros((T, T), F32)
        for b in range(T5_BUCKETS - 1):
            acc = jnp.where(bucket == b, tab_ref[b, h] - last, acc)
        o_ref[h, 0] = acc


def t5_tiles(tab, *, T, n_near):
    H = tab.shape[1]
    return pl.pallas_call(
        functools.partial(_t5_tiles_kernel, T=T, n_heads=H),
        grid=(n_near,),
        in_specs=[pl.BlockSpec(memory_space=pltpu.SMEM)],
        out_specs=pl.BlockSpec((H, 1, T, T), lambda d: (0, d, 0, 0)),
        out_shape=jax.ShapeDtypeStruct((H, n_near, T, T), F32),
        compiler_params=_cparams("parallel"),
        name="t5_tiles",
    )(tab)


def _n_near(T):
    return (T5_MAX_DISTANCE - 1 + T - 1) // T + 1


def _kmean_kernel(k_ref, o_ref, *, blk):
    k = k_ref[...].astype(F32)
    nb = k.shape[0] // blk
    o_ref[...] = k.reshape(nb, blk, LANE).sum(axis=1) * (1.0 / blk)


def moba_kmean(hs, k_slot0, S):
    nb = S // MOBA_BLOCK
    return pl.pallas_call(
        functools.partial(_kmean_kernel, blk=MOBA_BLOCK),
        grid=(A_HEADS,),
        in_specs=[pl.BlockSpec((None, S, LANE), lambda h: (k_slot0 + h, 0, 0))],
        out_specs=pl.BlockSpec((None, nb, LANE), lambda h: (h, 0, 0)),
        out_shape=jax.ShapeDtypeStruct((A_HEADS, nb, LANE), F32),
        compiler_params=_cparams("parallel"),
        name="moba_kmean",
    )(hs)


def _moba_select_kernel(q_ref, km_ref, o_ref, *, topk):
    own = pl.program_id(1)
    q = q_ref[...].astype(F32)
    gate = lax.dot_general(q, km_ref[...], _NT, precision=lax.Precision.HIGHEST,
                           preferred_element_type=F32)
    lane = lax.broadcasted_iota(jnp.int32, gate.shape, 1)
    nbp = gate.shape[1]
    gate = jnp.where(lane < own, gate, -jnp.inf)
    mask = jnp.where(lane == own, 0.0, NEG)
    for _ in range(topk):
        mx = jnp.max(gate, axis=-1, keepdims=True)
        idx = jnp.min(jnp.where(gate == mx, lane, nbp), axis=-1, keepdims=True)
        hit = lane == idx
        mask = jnp.where(hit, jnp.where(mx > -jnp.inf, 0.0, mask), mask)
        gate = jnp.where(hit, -jnp.inf, gate)
    o_ref[...] = mask.astype(o_ref.dtype)


def moba_select(hs, q_slot0, kmean_pad, S):
    nbp = kmean_pad.shape[1]
    return pl.pallas_call(
        functools.partial(_moba_select_kernel, topk=MOBA_TOPK),
        grid=(A_HEADS, S // MOBA_BLOCK),
        in_specs=[pl.BlockSpec((None, MOBA_BLOCK, LANE), lambda h, i: (q_slot0 + h, i, 0)),
                  pl.BlockSpec((None, nbp, LANE), lambda h, i: (h, 0, 0))],
        out_specs=pl.BlockSpec((None, MOBA_BLOCK, nbp), lambda h, i: (h, i, 0)),
        out_shape=jax.ShapeDtypeStruct((A_HEADS, S, nbp), BF16),
        compiler_params=_cparams("parallel", "parallel"),
        name="moba_select",
    )(hs, kmean_pad)


def _flash_kernel(*refs, mode, T, n_near, has_far, window, lam_init):
    it = iter(refs)
    q_ref, k_ref, v_ref = next(it), next(it), next(it)
    bias_ref = next(it) if mode in "ABC" else None
    sink_ref = next(it) if mode == "B" else None
    selb_ref = next(it) if mode == "A" else None
    lam_ref, g_ref = (next(it), next(it)) if mode == "C" else (None, None)
    o_ref = next(it)

    u = pl.program_id(0)
    qi = pl.program_id(1)
    q = q_ref[...]
    half = LANE // 2
    if mode == "A":
        qs = [jnp.concatenate([q, selb_ref[...]], axis=1)]
    elif mode == "B":
        z = jnp.zeros((T, half), q.dtype)
        qs = [jnp.concatenate([q[:, :half], z], axis=1), jnp.concatenate([q[:, half:], z], axis=1)]
    elif mode == "C":
        lane = lax.broadcasted_iota(jnp.int32, q.shape, 1)
        zq = jnp.zeros_like(q)
        qs = [jnp.where(lane < half, q, zq), jnp.where(lane >= half, q, zq)]
    else:
        qs = [q]
    n_sub = len(qs)
    dv = v_ref.shape[-1]

    def tile(j, carry, near):
        start = pl.multiple_of(j * T, T)
        kt = k_ref[pl.ds(start, T), :]
        vt = v_ref[pl.ds(start, T), :]
        if mode == "A":
            blk = lax.broadcasted_iota(jnp.int32, (T, selb_ref.shape[-1]), 1)
            onehot = jnp.where(blk == j, 1.0, 0.0).astype(kt.dtype)
            kt = jnp.concatenate([kt, onehot], axis=1)
        if near:
            dj = qi - j
            row = lax.broadcasted_iota(jnp.int32, (T, T), 0)
            col = lax.broadcasted_iota(jnp.int32, (T, T), 1)
            dist = row - col + dj * T
            valid = dist >= 0
            if window is not None:
                valid = valid & (dist < window)
        out = []
        for c in range(n_sub):
            m, l, acc = carry[c]
            s = lax.dot_general(qs[c], kt, _NT, preferred_element_type=F32)
            if near:
                if bias_ref is not None:
                    s = s + bias_ref[c if bias_ref.shape[0] > 1 else 0, dj]
                s = jnp.where(valid, s, NEG)
            m_new = jnp.maximum(m, jnp.max(s, axis=-1, keepdims=True))
            alpha = jnp.exp(m - m_new)
            p = jnp.exp(s - m_new)
            l = alpha * l + jnp.sum(p, axis=-1, keepdims=True)
            acc = alpha * acc + jnp.dot(p.astype(vt.dtype), vt, preferred_element_type=F32)
            out.append((m_new, l, acc))
        return tuple(out)

    init = []
    for c in range(n_sub):
        if sink_ref is not None:
            m0 = jnp.full((T, 1), sink_ref[u, c], F32)
            l0 = jnp.ones((T, 1), F32)
        else:
            m0 = jnp.full((T, 1), NEG, F32)
            l0 = jnp.zeros((T, 1), F32)
        init.append((m0, l0, jnp.zeros((T, dv), F32)))
    carry = tuple(init)

    near_lo = jnp.maximum(qi - n_near + 1, 0)
    if has_far:
        carry = lax.fori_loop(0, near_lo, lambda j, c: tile(j, c, False), carry)
    carry = lax.fori_loop(near_lo, qi + 1, lambda j, c: tile(j, c, True), carry)

    outs = [acc / l for (_, l, acc) in carry]
    if mode == "B":
        o = jnp.concatenate([outs[0][:, :half], outs[1][:, :half]], axis=1)
    elif mode == "C":
        lp = lam_ref[...]
        lam = (jnp.exp(jnp.sum(lp[0:1] * lp[1:2], axis=-1, keepdims=True))
               - jnp.exp(jnp.sum(lp[2:3] * lp[3:4], axis=-1, keepdims=True)) + lam_init)
        o = outs[0] - lam * outs[1]
        o = o * lax.rsqrt(jnp.mean(o * o, axis=-1, keepdims=True) + RMS_EPS) * g_ref[...]
        o = o * (1.0 - lam_init)
    else:
        o = outs[0]
    o_ref[...] = o.astype(o_ref.dtype)


def _flash_pipe_kernel(*refs, mode, T, n_near, lam_init):
    it = iter(refs)
    q_ref, k_ref, v_ref = next(it), next(it), next(it)
    bias_ref = next(it) if mode in "AC" else None
    selb_ref = next(it) if mode == "A" else None
    lam_ref, g_ref = (next(it), next(it)) if mode == "C" else (None, None)
    o_ref = next(it)
    s_scr, p_scr, m_scr, l_scr, acc_scr = next(it), next(it), next(it), next(it), next(it)

    qi = pl.program_id(1)
    q = q_ref[...]
    half = LANE // 2
    if mode == "A":
        qs = [jnp.concatenate([q, selb_ref[...]], axis=1)]
    elif mode == "C":
        lane = lax.broadcasted_iota(jnp.int32, q.shape, 1)
        zq = jnp.zeros_like(q)
        qs = [jnp.where(lane < half, q, zq), jnp.where(lane >= half, q, zq)]
    else:
        qs = [q]
    n_sub = len(qs)

    def k_tile(j):
        kt = k_ref[pl.ds(pl.multiple_of(j * T, T), T), :]
        if mode == "A":
            n_blk = selb_ref.shape[-1]
            blk = lax.broadcasted_iota(jnp.int32, (T, n_blk), 1)
            row = lax.broadcasted_iota(jnp.int32, (T, n_blk), 0)
            own = j * (T // MOBA_BLOCK) + row // MOBA_BLOCK
            kt = jnp.concatenate([kt, jnp.where(blk == own, 1.0, 0.0).astype(kt.dtype)], axis=1)
        return kt

    def v_tile(j):
        return v_ref[pl.ds(pl.multiple_of(j * T, T), T), :]

    def scores(c, kt):
        return lax.dot_general(qs[c], kt, _NT, preferred_element_type=F32)

    kt0 = k_tile(0)
    for c in range(n_sub):
        s_scr[0, c] = scores(c, kt0)
        p_scr[1, c] = jnp.zeros((T, T), p_scr.dtype)
        m_scr[c] = jnp.full((T, 1), NEG, F32)
        l_scr[c] = jnp.zeros((T, 1), F32)
        acc_scr[c] = jnp.zeros(acc_scr.shape[1:], F32)

    def step(j, near):
        slot = lax.rem(j, 2)
        nslot = 1 - slot
        kt_next = k_tile(jnp.minimum(j + 1, qi))
        vt_prev = v_tile(jnp.maximum(j - 1, 0))
        if near:
            dj = qi - j
            row = lax.broadcasted_iota(jnp.int32, (T, T), 0)
            col = lax.broadcasted_iota(jnp.int32, (T, T), 1)
            valid = row - col + dj * T >= 0
        for c in range(n_sub):
            pv = jnp.dot(p_scr[nslot, c], vt_prev, preferred_element_type=F32)
            s = s_scr[slot, c]
            s_scr[nslot, c] = scores(c, kt_next)
            if near:
                if bias_ref is not None:
                    s = s + bias_ref[0, dj]
                s = jnp.where(valid, s, NEG)
            m_prev = m_scr[c]
            m_new = jnp.maximum(m_prev, jnp.max(s, axis=-1, keepdims=True))
            alpha = jnp.exp(m_prev - m_new)
            p = jnp.exp(s - m_new)
            p_scr[slot, c] = p.astype(p_scr.dtype)
            l_scr[c] = alpha * l_scr[c] + jnp.sum(p, axis=-1, keepdims=True)
            acc_scr[c] = alpha * (acc_scr[c] + pv)
            m_scr[c] = m_new

    def far_body(j, _):
        step(j, False)
        return 0

    def near_body(j, _):
        step(j, True)
        return 0

    near_lo = jnp.maximum(qi - n_near + 1, 0)
    lax.fori_loop(0, near_lo, far_body, 0)
    lax.fori_loop(near_lo, qi + 1, near_body, 0)

    last = lax.rem(qi, 2)
    vt = v_tile(qi)
    outs = []
    for c in range(n_sub):
        acc = acc_scr[c] + jnp.dot(p_scr[last, c], vt, preferred_element_type=F32)
        outs.append(acc / l_scr[c])
    if mode == "C":
        lp = lam_ref[...]
        lam = (jnp.exp(jnp.sum(lp[0:1] * lp[1:2], axis=-1, keepdims=True))
               - jnp.exp(jnp.sum(lp[2:3] * lp[3:4], axis=-1, keepdims=True)) + lam_init)
        o = outs[0] - lam * outs[1]
        o = o * lax.rsqrt(jnp.mean(o * o, axis=-1, keepdims=True) + RMS_EPS) * g_ref[...]
        o = o * (1.0 - lam_init)
    else:
        o = outs[0]
    o_ref[...] = o.astype(o_ref.dtype)


def flash_pipe(mode, q_arr, k_arr, v_arr, *, S, T, q_slot, k_slot, v_slot, n_units=4, bias=None,
               selb=None, lam_params=None, subln_g=None, lam_init=0.0):
    dq = q_arr.shape[-1]
    dk = k_arr.shape[-1]
    dv = v_arr.shape[-1]
    n_near = 1 if mode == "D" else _n_near(T)
    n_sub = 2 if mode == "C" else 1
    in_specs = [pl.BlockSpec((None, T, dq), lambda u, i: (q_slot(u), i, 0)),
                pl.BlockSpec((None, S, dk), lambda u, i: (k_slot(u), 0, 0)),
                pl.BlockSpec((None, S, dv), lambda u, i: (v_slot(u), 0, 0))]
    args = [q_arr, k_arr, v_arr]
    if bias is not None:
        in_specs.append(pl.BlockSpec((None,) + bias.shape[1:], lambda u, i: (u, 0, 0, 0, 0)))
        args.append(bias)
    if selb is not None:
        in_specs.append(pl.BlockSpec((None, T, selb.shape[-1]), lambda u, i: (u, i, 0)))
        args.append(selb)
    if lam_params is not None:
        in_specs.append(pl.BlockSpec(lam_params.shape, lambda u, i: (0, 0)))
        in_specs.append(pl.BlockSpec(subln_g.shape, lambda u, i: (0, 0)))
        args += [lam_params, subln_g]
    return pl.pallas_call(
        functools.partial(_flash_pipe_kernel, mode=mode, T=T, n_near=n_near, lam_init=lam_init),
        grid=(n_units, S // T),
        in_specs=in_specs,
        out_specs=pl.BlockSpec((None, T, LANE), lambda u, i: (u, i, 0)),
        out_shape=jax.ShapeDtypeStruct((n_units, S, LANE), BF16),
        scratch_shapes=[pltpu.VMEM((2, n_sub, T, T), F32), pltpu.VMEM((2, n_sub, T, T), BF16),
                        pltpu.VMEM((n_sub, T, 1), F32), pltpu.VMEM((n_sub, T, 1), F32),
                        pltpu.VMEM((n_sub, T, dv), F32)],
        compiler_params=_cparams("parallel", "arbitrary"),
        name="flashp_" + mode,
    )(*args)


def flash(mode, q_arr, k_arr, v_arr, *, S, T, q_slot, k_slot, v_slot, n_units=4, bias=None,
          sinks=None, selb=None, lam_params=None, subln_g=None, lam_init=0.0):
    dq = q_arr.shape[-1]
    dk = k_arr.shape[-1]
    dv = v_arr.shape[-1]
    if mode == "B":
        n_near, has_far, window = 2, False, SWA_WINDOW
    elif mode == "D":
        n_near, has_far, window = 1, True, None
    else:
        n_near, has_far, window = _n_near(T), True, None
    in_specs = [pl.BlockSpec((None, T, dq), lambda u, i: (q_slot(u), i, 0)),
                pl.BlockSpec((None, S, dk), lambda u, i: (k_slot(u), 0, 0)),
                pl.BlockSpec((None, S, dv), lambda u, i: (v_slot(u), 0, 0))]
    args = [q_arr, k_arr, v_arr]
    if bias is not None:
        in_specs.append(pl.BlockSpec((None,) + bias.shape[1:], lambda u, i: (u, 0, 0, 0, 0)))
        args.append(bias)
    if sinks is not None:
        in_specs.append(pl.BlockSpec(memory_space=pltpu.SMEM))
        args.append(sinks)
    if selb is not None:
        in_specs.append(pl.BlockSpec((None, T, selb.shape[-1]), lambda u, i: (u, i, 0)))
        args.append(selb)
    if lam_params is not None:
        in_specs.append(pl.BlockSpec(lam_params.shape, lambda u, i: (0, 0)))
        in_specs.append(pl.BlockSpec(subln_g.shape, lambda u, i: (0, 0)))
        args += [lam_params, subln_g]
    return pl.pallas_call(
        functools.partial(_flash_kernel, mode=mode, T=T, n_near=n_near, has_far=has_far,
                          window=window, lam_init=lam_init),
        grid=(n_units, S // T),
        in_specs=in_specs,
        out_specs=pl.BlockSpec((None, T, LANE), lambda u, i: (u, i, 0)),
        out_shape=jax.ShapeDtypeStruct((n_units, S, LANE), BF16),
        compiler_params=_cparams("parallel", "arbitrary"),
        name="flash_" + mode,
    )(*args)


def _rms(x, g):
    return x * lax.rsqrt(jnp.mean(x * x, axis=-1, keepdims=True) + RMS_EPS) * g


def _mla_prep_kernel(hd_ref, gq_ref, gkv_ref, wqa_ref, wqb_ref, wkv_ref, cos_ref, sin_ref,
                     qo_ref, ko_ref, vo_ref, *, scale):
    cosm = cos_ref[...]
    sinm = sin_ref[...]
    cq = jnp.concatenate([hd_ref[c] for c in range(4)], axis=1)
    cqn = _rms(cq, gq_ref[...]).astype(BF16)
    qa = jnp.dot(cqn, wqa_ref[...], preferred_element_type=F32)
    qb = jnp.dot(cqn, wqb_ref[...], preferred_element_type=F32)
    w = 2 * LANE
    for h in range(D_HEADS):
        qe = (qa[:, h * w:(h + 1) * w] * cosm + qb[:, h * w:(h + 1) * w] * sinm) * scale
        qo_ref[h] = qe.astype(qo_ref.dtype)
    ckv = jnp.concatenate([hd_ref[4], hd_ref[5]], axis=1)
    ckvn = _rms(ckv, gkv_ref[...]).astype(BF16)
    kv = jnp.dot(ckvn, wkv_ref[...], preferred_element_type=F32)
    kpe = hd_ref[6] * cosm[:, LANE:] + hd_ref[7] * sinm[:, LANE:]
    for h in range(D_HEADS):
        ko_ref[h] = jnp.concatenate([kv[:, h * LANE:(h + 1) * LANE], kpe], axis=1).astype(ko_ref.dtype)
        vo_ref[h] = kv[:, (D_HEADS + h) * LANE:(D_HEADS + h + 1) * LANE].astype(vo_ref.dtype)


def mla_prep(hd, gq, gkv, wqa, wqb, wkv, cosm, sinm, *, S, tm):
    w = 2 * LANE
    full = lambda a: pl.BlockSpec(a.shape, lambda i: (0,) * a.ndim)
    return pl.pallas_call(
        functools.partial(_mla_prep_kernel, scale=(D_NOPE + D_ROPE) ** -0.5),
        grid=(S // tm,),
        in_specs=[pl.BlockSpec((8, tm, LANE), lambda i: (0, i, 0)),
                  full(gq), full(gkv), full(wqa), full(wqb), full(wkv),
                  pl.BlockSpec((tm, w), lambda i: (i, 0)),
                  pl.BlockSpec((tm, w), lambda i: (i, 0))],
        out_specs=[pl.BlockSpec((D_HEADS, tm, w), lambda i: (0, i, 0)),
                   pl.BlockSpec((D_HEADS, tm, w), lambda i: (0, i, 0)),
                   pl.BlockSpec((D_HEADS, tm, LANE), lambda i: (0, i, 0))],
        out_shape=[jax.ShapeDtypeStruct((D_HEADS, S, w), BF16),
                   jax.ShapeDtypeStruct((D_HEADS, S, w), BF16),
                   jax.ShapeDtypeStruct((D_HEADS, S, LANE), BF16)],
        compiler_params=_cparams("parallel"),
        name="mla_prep",
    )(hd, gq, gkv, wqa, wqb, wkv, cosm, sinm)


def _merge_kernel(x_ref, oa_ref, ob_ref, oc_ref, od_ref, wg_ref, bg_ref, wbr_ref, o_ref):
    x = x_ref[...]
    acc = None
    for n, o in enumerate((oa_ref, ob_ref, oc_ref, od_ref)):
        g = jnp.dot(x, wg_ref[n], preferred_element_type=F32) + bg_ref[n]
        ocat = jnp.concatenate([o[c] for c in range(4)], axis=1)
        b = jnp.dot(ocat, wbr_ref[n], preferred_element_type=F32)
        t = jax.nn.sigmoid(g) * b
        acc = t if acc is None else acc + t
    o_ref[...] = acc.astype(o_ref.dtype)


def merge(x_bf, outs, wg, bg, wbr, *, tm, tn):
    S, D = x_bf.shape
    o_spec = pl.BlockSpec((4, tm, LANE), lambda j, i: (0, i, 0))
    return pl.pallas_call(
        _merge_kernel,
        grid=(D // tn, S // tm),
        in_specs=[pl.BlockSpec((tm, D), lambda j, i: (i, 0)), o_spec, o_spec, o_spec, o_spec,
                  pl.BlockSpec((N_BRANCHES, D, tn), lambda j, i: (0, 0, j)),
                  pl.BlockSpec((N_BRANCHES, 1, tn), lambda j, i: (0, 0, j)),
                  pl.BlockSpec((N_BRANCHES, BRANCH_W, tn), lambda j, i: (0, 0, j))],
        out_specs=pl.BlockSpec((tm, tn), lambda j, i: (i, j)),
        out_shape=jax.ShapeDtypeStruct((S, D), BF16),
        compiler_params=_cparams("parallel", "parallel"),
        name="merge",
    )(x_bf, *outs, wg, bg, wbr)


def _layer_norm(y, g, b):
    mu = jnp.mean(y, axis=-1, keepdims=True)
    var = jnp.mean(jnp.square(y - mu), axis=-1, keepdims=True)
    return (y - mu) * lax.rsqrt(var + LN_EPS) * g + b


def _out_ln_kernel(x_ref, m_ref, w_ref, g_ref, b_ref, of_ref, ob_ref, *, alpha):
    y = alpha * x_ref[...] + jnp.dot(m_ref[...], w_ref[...], preferred_element_type=F32)
    o = _layer_norm(y, g_ref[...], b_ref[...])
    of_ref[...] = o
    ob_ref[...] = o.astype(ob_ref.dtype)


def out_ln(x, merged, w_out, g, b, *, alpha, tm):
    S, D = x.shape
    row = pl.BlockSpec((tm, D), lambda i: (i, 0))
    vec = pl.BlockSpec((1, D), lambda i: (0, 0))
    return pl.pallas_call(
        functools.partial(_out_ln_kernel, alpha=alpha),
        grid=(S // tm,),
        in_specs=[row, row, pl.BlockSpec((D, D), lambda i: (0, 0)), vec, vec],
        out_specs=[row, row],
        out_shape=[jax.ShapeDtypeStruct((S, D), F32), jax.ShapeDtypeStruct((S, D), BF16)],
        compiler_params=_cparams("parallel"),
        name="out_ln",
    )(x, merged, w_out, g, b)


def _add_ln_kernel(x_ref, y_ref, g_ref, b_ref, of_ref, ob_ref, *, alpha):
    o = _layer_norm(alpha * x_ref[...] + y_ref[...], g_ref[...], b_ref[...])
    of_ref[...] = o
    ob_ref[...] = o.astype(ob_ref.dtype)


def add_ln(x, y, g, b, *, alpha, tm):
    S, D = x.shape
    row = pl.BlockSpec((tm, D), lambda i: (i, 0))
    vec = pl.BlockSpec((1, D), lambda i: (0, 0))
    return pl.pallas_call(
        functools.partial(_add_ln_kernel, alpha=alpha),
        grid=(S // tm,),
        in_specs=[row, row, vec, vec],
        out_specs=[row, row],
        out_shape=[jax.ShapeDtypeStruct((S, D), F32), jax.ShapeDtypeStruct((S, D), BF16)],
        compiler_params=_cparams("parallel"),
        name="add_ln",
    )(x, y, g, b)


def _top_sorted(s, k):
    vals = []
    cur = s
    for r in range(k):
        m = jnp.max(cur, axis=0, keepdims=True)
        vals.append(m)
        if r + 1 < k:
            cur = jnp.where(cur == m, -jnp.inf, cur)
    return vals


def _peer_route_kernel(q_ref, keys_ref, s0_ref, s1_ref, e1_ref, c_ref, tau_ref):
    K = PEER_TOPK

    def head(h, _):
        s0 = lax.dot_general(keys_ref[h, 0], q_ref[2 * h], _NT, precision=lax.Precision.HIGHEST,
                             preferred_element_type=F32)
        s1 = lax.dot_general(keys_ref[h, 1], q_ref[2 * h + 1], _NT, precision=lax.Precision.HIGHEST,
                             preferred_element_type=F32)
        v0 = _top_sorted(s0, K)
        v1 = jnp.concatenate(_top_sorted(s1, K), axis=0)
        cands = [v0[a] + v1 for a in range(K)]
        z = None
        for r in range(K):
            m = functools.reduce(jnp.maximum, cands)
            m = jnp.max(m, axis=0, keepdims=True)
            if r == 0:
                top = m
                z = jnp.ones_like(m)
            else:
                z = z + jnp.exp(m - top)
            if r + 1 < K:
                cands = [jnp.where(c == m, -jnp.inf, c) for c in cands]
        s0_ref[h] = s0
        s1_ref[h] = s1
        e1_ref[h] = jnp.exp(s1 - v1[0:1])
        c_ref[h] = jnp.exp(s0 - v0[0]) / z
        tau_ref[pl.ds(h, 1), :] = m
        return 0

    lax.fori_loop(0, PEER_HEADS, head, 0)


def peer_route(q_slots, sub_keys, *, S, tq):
    arr = jax.ShapeDtypeStruct((PEER_HEADS, PEER_N_KEYS, S), F32)
    spec = pl.BlockSpec((PEER_HEADS, PEER_N_KEYS, tq), lambda i: (0, 0, i))
    return pl.pallas_call(
        _peer_route_kernel,
        grid=(S // tq,),
        in_specs=[pl.BlockSpec((2 * PEER_HEADS, tq, LANE), lambda i: (0, i, 0)),
                  pl.BlockSpec(sub_keys.shape, lambda i: (0, 0, 0, 0))],
        out_specs=[spec, spec, spec, spec, pl.BlockSpec((PEER_HEADS, tq), lambda i: (0, i))],
        out_shape=[arr, arr, arr, arr, jax.ShapeDtypeStruct((PEER_HEADS, S), F32)],
        compiler_params=_cparams("parallel"),
        name="peer_route",
    )(q_slots, sub_keys)


def _peer_expert_kernel(x_ref, u_ref, v_ref, s0_ref, s1_ref, e1_ref, c_ref, tau_ref, y_ref):
    e = pl.program_id(1)
    n_rows = u_ref.shape[0] // PEER_N_KEYS

    @pl.when(e == 0)
    def _():
        y_ref[...] = jnp.zeros_like(y_ref)

    ht = lax.dot_general(u_ref[...], x_ref[...], _NT, preferred_element_type=F32)
    act = 0.5 * ht * (1.0 + lax.erf(ht * math.sqrt(0.5)))
    ws = []
    for r in range(n_rows):
        i = e * n_rows + r
        g = None
        for h in range(PEER_HEADS):
            ssum = s0_ref[h, pl.ds(i, 1), :] + s1_ref[h]
            t = jnp.where(ssum >= tau_ref[pl.ds(h, 1), :], e1_ref[h], 0.0) * c_ref[h, pl.ds(i, 1), :]
            g = t if g is None else g + t
        ws.append((g * act[r * PEER_N_KEYS:(r + 1) * PEER_N_KEYS, :]).astype(BF16))
    wt = jnp.concatenate(ws, axis=0) if n_rows > 1 else ws[0]
    y_ref[...] += lax.dot_general(wt, v_ref[...], _TN, preferred_element_type=F32)


def peer_experts(x_bf, u_bf, v_bf, s0, s1, e1, c, tau, *, tq, te):
    S, D = x_bf.shape
    E = u_bf.shape[0]
    rt = pl.BlockSpec((PEER_HEADS, PEER_N_KEYS, tq), lambda i, e: (0, 0, i))
    return pl.pallas_call(
        _peer_expert_kernel,
        grid=(S // tq, E // te),
        in_specs=[pl.BlockSpec((tq, D), lambda i, e: (i, 0)),
                  pl.BlockSpec((te, D), lambda i, e: (e, 0)),
                  pl.BlockSpec((te, D), lambda i, e: (e, 0)),
                  rt, rt, rt, rt,
                  pl.BlockSpec((PEER_HEADS, tq), lambda i, e: (0, i))],
        out_specs=pl.BlockSpec((tq, D), lambda i, e: (i, 0)),
        out_shape=jax.ShapeDtypeStruct((S, D), F32),
        compiler_params=_cparams("parallel", "arbitrary"),
        name="peer_experts",
    )(x_bf, u_bf, v_bf, s0, s1, e1, c, tau)


def _pad_cols(w, width):
    return jnp.pad(w, ((0, 0), (0, width - w.shape[1])))


def _rot_cols(w):
    h = w.shape[1] // 2
    return jnp.concatenate([-w[:, h:], w[:, :h]], axis=1)


def _split_w_in(w):
    widths = (512, 512, 512, 512, 128, 128, 512, 512, 512, D_Q_LORA, D_KV_LORA, D_ROPE)
    out, o = [], 0
    for n in widths:
        out.append(w[:, o:o + n])
        o += n
    return out


def _layer_weights(w_in, a_scale, b_scale, c_scale):
    (a_q, a_k, a_v, b_q, b_k, b_v, c_q, c_k, c_v, d_cq, d_ckv, d_kr) = _split_w_in(w_in)
    b_k0, b_k1 = b_k[:, :64], b_k[:, 64:]
    b_v0, b_v1 = b_v[:, :64], b_v[:, 64:]
    w_abc = jnp.concatenate(
        [a_q, a_k, a_v,
         b_q, _pad_cols(b_k0, LANE), _pad_cols(b_k1, LANE), _pad_cols(b_v0, LANE), _pad_cols(b_v1, LANE),
         c_q, c_k, c_v], axis=1).astype(BF16)
    ones = lambda n: jnp.ones((n,), F32)
    s_abc = jnp.concatenate([ones(512) * a_scale, ones(1024),
                             ones(512) * b_scale, ones(512),
                             ones(512) * c_scale, ones(1024)])[None, :]
    w_d = jnp.concatenate([d_cq, d_ckv, _pad_cols(d_kr, LANE), _pad_cols(_rot_cols(d_kr), LANE)],
                          axis=1).astype(BF16)
    return w_abc, s_abc, w_d


def _mla_weights(w_uq, w_ukv):
    wq = w_uq.reshape(D_Q_LORA, D_HEADS, D_NOPE + D_ROPE)
    nope, pe = wq[..., :D_NOPE], wq[..., D_NOPE:]
    pe_rot = jnp.concatenate([-pe[..., D_ROPE // 2:], pe[..., :D_ROPE // 2]], axis=-1)
    z64 = jnp.zeros((D_Q_LORA, D_HEADS, LANE - D_ROPE), F32)
    z128 = jnp.zeros((D_Q_LORA, D_HEADS, D_NOPE), F32)
    wqa = jnp.concatenate([nope, pe, z64], axis=-1).reshape(D_Q_LORA, -1).astype(BF16)
    wqb = jnp.concatenate([z128, pe_rot, z64], axis=-1).reshape(D_Q_LORA, -1).astype(BF16)
    wkv = w_ukv.reshape(D_KV_LORA, D_HEADS, D_NOPE + D_V)
    wkv = jnp.concatenate([wkv[..., :D_NOPE].reshape(D_KV_LORA, -1),
                           wkv[..., D_NOPE:].reshape(D_KV_LORA, -1)], axis=1).astype(BF16)
    return wqa, wqb, wkv


def _rope_tables(positions):
    half = D_ROPE // 2
    inv = ROPE_THETA ** (-jnp.arange(half, dtype=F32) / half)
    ang = positions.astype(F32)[..., None] * inv
    cos, sin = jnp.cos(ang), jnp.sin(ang)
    S = positions.shape[0]
    one = jnp.ones((S, D_NOPE), F32)
    z = jnp.zeros((S, LANE - D_ROPE), F32)
    cosm = jnp.concatenate([one, cos, cos, z], axis=1)
    sinm = jnp.concatenate([jnp.zeros((S, D_NOPE), F32), sin, sin, z], axis=1)
    return cosm, sinm


def kernel(x, positions, t5_table, w_in, w_gate, b_gate, w_branch, w_out, swa_sinks, diff_lambda_q1, diff_lambda_k1, diff_lambda_q2, diff_lambda_k2, diff_subln_g, mla_q_norm_g, mla_kv_norm_g, mla_w_uq, mla_w_ukv, ln1_g, ln1_b, peer_w_q, peer_sub_keys, peer_u, peer_v, ln2_g, ln2_b):
    B, S, D = x.shape
    assert B == 1 and D == D_MODEL and S % MOBA_BLOCK == 0
    depth = w_in.shape[0]
    alpha = (2.0 * depth) ** 0.25
    tm = min(512, S)
    T = min(512, S)
    TB = SWA_WINDOW

    tab_a = t5_table[:, :A_HEADS]
    tab_b = t5_table[:, A_HEADS:A_HEADS + B_HEADS]
    tab_c = t5_table[:, A_HEADS + B_HEADS:]
    bias_a = t5_tiles(tab_a, T=T, n_near=_n_near(T))[:, None]
    bias_c = t5_tiles(tab_c, T=T, n_near=_n_near(T))[:, None]
    bias_b = t5_tiles(tab_b, T=TB, n_near=2).reshape(B_HEADS // 2, 2, 2, TB, TB)
    cosm, sinm = _rope_tables(positions[0])

    xf = x[0]
    xb = xf.astype(BF16)
    for layer in range(depth):
        lam_init = 0.8 - 0.6 * math.exp(-0.3 * layer)
        w_abc, s_abc, w_d = _layer_weights(w_in[layer], 128 ** -0.5, B_HEAD_DIM ** -0.5, C_HEAD_DIM ** -0.5)
        hs = proj_slots(xb, w_abc, s_abc, BF16, tm=tm, slots_per_step=8)
        hd = proj_slots(xb, w_d, jnp.ones((1, w_d.shape[1]), F32), F32, tm=tm, slots_per_step=8)

        nb = S // MOBA_BLOCK
        nbp = -(-nb // LANE) * LANE
        kmean = jnp.pad(moba_kmean(hs, 4, S), ((0, 0), (0, nbp - nb), (0, 0)))
        selb = moba_select(hs, 0, kmean, S)
        o_a = flash_pipe("A", hs, hs, hs, S=S, T=T, q_slot=lambda u: u, k_slot=lambda u: 4 + u,
                    v_slot=lambda u: 8 + u, bias=bias_a, selb=selb)
        sinks = (swa_sinks[layer] - tab_b[T5_BUCKETS - 1]).reshape(B_HEADS // 2, 2)
        o_b = flash("B", hs, hs, hs, S=S, T=TB, q_slot=lambda u: 12 + u, k_slot=lambda u: 16 + u // 2,
                    v_slot=lambda u: 18 + u // 2, bias=bias_b, sinks=sinks)
        lam_params = jnp.stack([diff_lambda_q1[layer], diff_lambda_k1[layer],
                                diff_lambda_q2[layer], diff_lambda_k2[layer]])
        o_c = flash_pipe("C", hs, hs, hs, S=S, T=T, q_slot=lambda u: 20 + u, k_slot=lambda u: 24 + u,
                    v_slot=lambda u: 28 + u, bias=bias_c, lam_params=lam_params,
                    subln_g=diff_subln_g[layer][None, :], lam_init=lam_init)
        wqa, wqb, wkv = _mla_weights(mla_w_uq[layer], mla_w_ukv[layer])
        q_d, k_d, v_d = mla_prep(hd, mla_q_norm_g[layer][None, :], mla_kv_norm_g[layer][None, :],
                                 wqa, wqb, wkv, cosm, sinm, S=S, tm=tm)
        o_d = flash_pipe("D", q_d, k_d, v_d, S=S, T=T, q_slot=lambda u: u, k_slot=lambda u: u,
                    v_slot=lambda u: u)

        merged = merge(xb, (o_a, o_b, o_c, o_d),
                       w_gate[layer].astype(BF16), b_gate[layer][:, None, :], w_branch[layer].astype(BF16), tm=tm, tn=512)
        xf, xb = out_ln(xf, merged, w_out[layer].astype(BF16), ln1_g[layer][None, :],
                        ln1_b[layer][None, :], alpha=alpha, tm=tm)

        q_slots = proj_slots(xb, peer_w_q[layer].astype(BF16), jnp.ones((1, D), F32), F32,
                             tm=tm, slots_per_step=8)
        s0, s1, e1, c, tau = peer_route(q_slots, peer_sub_keys[layer], S=S, tq=LANE)
        y = peer_experts(xb, peer_u[layer].astype(BF16), peer_v[layer].astype(BF16),
                         s0, s1, e1, c, tau, tq=tm, te=512)
        xf, xb = add_ln(xf, y, ln2_g[layer][None, :], ln2_b[layer][None, :], alpha=alpha, tm=tm)
    return xf[None]
```

```python
import functools
import math

import jax
import jax.numpy as jnp
from jax import lax
from jax.experimental import pallas as pl
from jax.experimental.pallas import tpu as pltpu

F32 = jnp.float32
BF16 = jnp.bfloat16
LANE = 128
NEG = -1e30
LOG2E = math.log2(math.e)

D_MODEL = 2048
A_HEADS = 4
MOBA_BLOCK = 256
MOBA_TOPK = 3
B_HEADS = 8
B_KV_HEADS = 2
B_HEAD_DIM = 64
SWA_WINDOW = 128
C_HEADS = 4
C_HEAD_DIM = 64
D_HEADS = 4
D_Q_LORA = 512
D_KV_LORA = 256
D_NOPE = 128
D_ROPE = 64
D_V = 128
ROPE_THETA = 10000.0
N_BRANCHES = 4
BRANCH_W = 512
T5_BUCKETS = 32
T5_MAX_EXACT = T5_BUCKETS // 2
T5_MAX_DISTANCE = 1024
PEER_HEADS = 8
PEER_N_KEYS = 128
PEER_D_KEY = 128
PEER_TOPK = 16
LN_EPS = 1e-5
RMS_EPS = 1e-6

VMEM_LIMIT_MB = 56


def _cparams(*sem):
    return pltpu.CompilerParams(dimension_semantics=sem,
                                vmem_limit_bytes=VMEM_LIMIT_MB * 1024 * 1024)


_NT = (((1,), (1,)), ((), ()))
_TN = (((0,), (0,)), ((), ()))


def _proj_kernel(x_ref, w_ref, s_ref, o_ref):
    acc = jnp.dot(x_ref[...], w_ref[...], preferred_element_type=F32) * s_ref[...]
    for c in range(o_ref.shape[0]):
        o_ref[c] = acc[:, c * LANE:(c + 1) * LANE].astype(o_ref.dtype)


def proj_slots(x_bf, w_bf, colscale, out_dtype, *, tm, slots_per_step):
    S, K = x_bf.shape
    N = w_bf.shape[1]
    tn = slots_per_step * LANE
    return pl.pallas_call(
        _proj_kernel,
        grid=(N // tn, S // tm),
        in_specs=[pl.BlockSpec((tm, K), lambda j, i: (i, 0)),
                  pl.BlockSpec((K, tn), lambda j, i: (0, j)),
                  pl.BlockSpec((1, tn), lambda j, i: (0, j))],
        out_specs=pl.BlockSpec((slots_per_step, tm, LANE), lambda j, i: (j, i, 0)),
        out_shape=jax.ShapeDtypeStruct((N // LANE, S, LANE), out_dtype),
        compiler_params=_cparams("parallel", "parallel"),
        name="proj_slots",
    )(x_bf, w_bf, colscale)


def _t5_tiles_kernel(tab_ref, o_ref, *, T, n_heads):
    dj = pl.program_id(0)
    row = lax.broadcasted_iota(jnp.int32, (T, T), 0)
    col = lax.broadcasted_iota(jnp.int32, (T, T), 1)
    n = jnp.maximum(row - col + dj * T, 0)
    nf = jnp.maximum(n, 1).astype(F32)
    large = T5_MAX_EXACT + (jnp.log(nf / T5_MAX_EXACT) / math.log(T5_MAX_DISTANCE / T5_MAX_EXACT)
                            * (T5_BUCKETS - T5_MAX_EXACT)).astype(jnp.int32)
    large = jnp.minimum(large, T5_BUCKETS - 1)
    bucket = jnp.where(n < T5_MAX_EXACT, n, large)
    for h in range(n_heads):
        last = tab_ref[T5_BUCKETS - 1, h]
        acc = jnp.zeros((T, T), F32)
        for b in range(T5_BUCKETS - 1):
            acc = jnp.where(bucket == b, (tab_ref[b, h] - last) * LOG2E, acc)
        o_ref[h, 0] = acc


def t5_tiles(tab, *, T, n_near):
    H = tab.shape[1]
    return pl.pallas_call(
        functools.partial(_t5_tiles_kernel, T=T, n_heads=H),
        grid=(n_near,),
        in_specs=[pl.BlockSpec(memory_space=pltpu.SMEM)],
        out_specs=pl.BlockSpec((H, 1, T, T), lambda d: (0, d, 0, 0)),
        out_shape=jax.ShapeDtypeStruct((H, n_near, T, T), F32),
        compiler_params=_cparams("parallel"),
        name="t5_tiles",
    )(tab)


def _n_near(T):
    return (T5_MAX_DISTANCE - 1 + T - 1) // T + 1


def _kmean_kernel(k_ref, o_ref, *, blk):
    k = k_ref[...].astype(F32)
    nb = k.shape[0] // blk
    o_ref[...] = k.reshape(nb, blk, LANE).sum(axis=1) * (1.0 / blk)


def moba_kmean(hs, k_slot0, S):
    nb = S // MOBA_BLOCK
    return pl.pallas_call(
        functools.partial(_kmean_kernel, blk=MOBA_BLOCK),
        grid=(A_HEADS,),
        in_specs=[pl.BlockSpec((None, S, LANE), lambda h: (k_slot0 + h, 0, 0))],
        out_specs=pl.BlockSpec((None, nb, LANE), lambda h: (h, 0, 0)),
        out_shape=jax.ShapeDtypeStruct((A_HEADS, nb, LANE), F32),
        compiler_params=_cparams("parallel"),
        name="moba_kmean",
    )(hs)


def _moba_select_kernel(q_ref, km_ref, o_ref, *, topk):
    own = pl.program_id(1)
    q = q_ref[...].astype(F32)
    gate = lax.dot_general(q, km_ref[...], _NT, precision=lax.Precision.HIGHEST,
                           preferred_element_type=F32)
    lane = lax.broadcasted_iota(jnp.int32, gate.shape, 1)
    nbp = gate.shape[1]
    gate = jnp.where(lane < own, gate, -jnp.inf)
    mask = jnp.where(lane == own, 0.0, NEG)
    for _ in range(topk):
        mx = jnp.max(gate, axis=-1, keepdims=True)
        idx = jnp.min(jnp.where(gate == mx, lane, nbp), axis=-1, keepdims=True)
        hit = lane == idx
        mask = jnp.where(hit, jnp.where(mx > -jnp.inf, 0.0, mask), mask)
        gate = jnp.where(hit, -jnp.inf, gate)
    o_ref[...] = mask.astype(o_ref.dtype)


def moba_select(hs, q_slot0, kmean_pad, S):
    nbp = kmean_pad.shape[1]
    return pl.pallas_call(
        functools.partial(_moba_select_kernel, topk=MOBA_TOPK),
        grid=(A_HEADS, S // MOBA_BLOCK),
        in_specs=[pl.BlockSpec((None, MOBA_BLOCK, LANE), lambda h, i: (q_slot0 + h, i, 0)),
                  pl.BlockSpec((None, nbp, LANE), lambda h, i: (h, 0, 0))],
        out_specs=pl.BlockSpec((None, MOBA_BLOCK, nbp), lambda h, i: (h, i, 0)),
        out_shape=jax.ShapeDtypeStruct((A_HEADS, S, nbp), BF16),
        compiler_params=_cparams("parallel", "parallel"),
        name="moba_select",
    )(hs, kmean_pad)


def _flash_kernel(*refs, mode, T, n_near, has_far, window, lam_init):
    it = iter(refs)
    q_ref, k_ref, v_ref = next(it), next(it), next(it)
    bias_ref = next(it) if mode in "ABC" else None
    sink_ref = next(it) if mode == "B" else None
    selb_ref = next(it) if mode == "A" else None
    lam_ref, g_ref = (next(it), next(it)) if mode == "C" else (None, None)
    o_ref = next(it)

    u = pl.program_id(0)
    qi = pl.program_id(1)
    q = q_ref[...]
    half = LANE // 2
    if mode == "A":
        qs = [jnp.concatenate([q, selb_ref[...]], axis=1)]
    elif mode == "B":
        z = jnp.zeros((T, half), q.dtype)
        qs = [jnp.concatenate([q[:, :half], z], axis=1), jnp.concatenate([q[:, half:], z], axis=1)]
    elif mode == "C":
        lane = lax.broadcasted_iota(jnp.int32, q.shape, 1)
        zq = jnp.zeros_like(q)
        qs = [jnp.where(lane < half, q, zq), jnp.where(lane >= half, q, zq)]
    else:
        qs = [q]
    n_sub = len(qs)
    dv = v_ref.shape[-1]

    def tile(j, carry, near):
        start = pl.multiple_of(j * T, T)
        kt = k_ref[pl.ds(start, T), :]
        vt = v_ref[pl.ds(start, T), :]
        if mode == "A":
            blk = lax.broadcasted_iota(jnp.int32, (T, selb_ref.shape[-1]), 1)
            onehot = jnp.where(blk == j, 1.0, 0.0).astype(kt.dtype)
            kt = jnp.concatenate([kt, onehot], axis=1)
        if near:
            dj = qi - j
            row = lax.broadcasted_iota(jnp.int32, (T, T), 0)
            col = lax.broadcasted_iota(jnp.int32, (T, T), 1)
            dist = row - col + dj * T
            valid = dist >= 0
            if window is not None:
                valid = valid & (dist < window)
        out = []
        for c in range(n_sub):
            m, l, acc = carry[c]
            s = lax.dot_general(qs[c], kt, _NT, preferred_element_type=F32)
            if near:
                if bias_ref is not None:
                    s = s + bias_ref[c if bias_ref.shape[0] > 1 else 0, dj]
                s = jnp.where(valid, s, NEG)
            m_new = jnp.maximum(m, jnp.max(s, axis=-1, keepdims=True))
            alpha = jnp.exp2(m - m_new)
            p = jnp.exp2(s - m_new)
            l = alpha * l + jnp.sum(p, axis=-1, keepdims=True)
            acc = alpha * acc + jnp.dot(p.astype(vt.dtype), vt, preferred_element_type=F32)
            out.append((m_new, l, acc))
        return tuple(out)

    init = []
    for c in range(n_sub):
        if sink_ref is not None:
            m0 = jnp.full((T, 1), sink_ref[u, c], F32)
            l0 = jnp.ones((T, 1), F32)
        else:
            m0 = jnp.full((T, 1), NEG, F32)
            l0 = jnp.zeros((T, 1), F32)
        init.append((m0, l0, jnp.zeros((T, dv), F32)))
    carry = tuple(init)

    near_lo = jnp.maximum(qi - n_near + 1, 0)
    if has_far:
        carry = lax.fori_loop(0, near_lo, lambda j, c: tile(j, c, False), carry)
    carry = lax.fori_loop(near_lo, qi + 1, lambda j, c: tile(j, c, True), carry)

    outs = [acc / l for (_, l, acc) in carry]
    if mode == "B":
        o = jnp.concatenate([outs[0][:, :half], outs[1][:, :half]], axis=1)
    elif mode == "C":
        lp = lam_ref[...]
        lam = (jnp.exp(jnp.sum(lp[0:1] * lp[1:2], axis=-1, keepdims=True))
               - jnp.exp(jnp.sum(lp[2:3] * lp[3:4], axis=-1, keepdims=True)) + lam_init)
        o = outs[0] - lam * outs[1]
        o = o * lax.rsqrt(jnp.mean(o * o, axis=-1, keepdims=True) + RMS_EPS) * g_ref[...]
        o = o * (1.0 - lam_init)
    else:
        o = outs[0]
    o_ref[...] = o.astype(o_ref.dtype)


def _flash_pipe_kernel(*refs, mode, T, n_near, lam_init):
    it = iter(refs)
    q_ref, k_ref, v_ref = next(it), next(it), next(it)
    bias_ref = next(it) if mode in "AC" else None
    selb_ref = next(it) if mode == "A" else None
    lam_ref, g_ref = (next(it), next(it)) if mode == "C" else (None, None)
    o_ref = next(it)
    m_scr, l_scr, acc_scr = next(it), next(it), next(it)
    rest = list(it)
    s_scr, p_scr = rest[:len(rest) // 2], rest[len(rest) // 2:]

    qi = pl.program_id(1)
    q = q_ref[...]
    half = LANE // 2
    if mode == "A":
        qs = [jnp.concatenate([q, selb_ref[...]], axis=1)]
    elif mode == "C":
        lane = lax.broadcasted_iota(jnp.int32, q.shape, 1)
        zq = jnp.zeros_like(q)
        qs = [jnp.where(lane < half, q, zq), jnp.where(lane >= half, q, zq)]
    else:
        qs = [q]
    n_sub = len(qs)

    def k_tile(j):
        kt = k_ref[pl.ds(pl.multiple_of(j * T, T), T), :]
        if mode == "A":
            n_blk = selb_ref.shape[-1]
            blk = lax.broadcasted_iota(jnp.int32, (T, n_blk), 1)
            row = lax.broadcasted_iota(jnp.int32, (T, n_blk), 0)
            own = j * (T // MOBA_BLOCK) + row // MOBA_BLOCK
            kt = jnp.concatenate([kt, jnp.where(blk == own, 1.0, 0.0).astype(kt.dtype)], axis=1)
        return kt

    def v_tile(j):
        return v_ref[pl.ds(pl.multiple_of(j * T, T), T), :]

    def scores(c, kt):
        return lax.dot_general(qs[c], kt, _NT, preferred_element_type=F32)

    kt0 = k_tile(0)
    for c in range(n_sub):
        s_scr[c][0] = scores(c, kt0)
        p_scr[c][1] = jnp.zeros((T, T), p_scr[c].dtype)
        m_scr[c] = jnp.full((T, LANE), NEG, F32)
        l_scr[c] = jnp.zeros((T, LANE), F32)
        acc_scr[c] = jnp.zeros(acc_scr.shape[1:], F32)

    def step(j, near):
        slot = lax.rem(j, 2)
        nslot = 1 - slot
        kt_next = k_tile(jnp.minimum(j + 1, qi))
        vt_prev = v_tile(jnp.maximum(j - 1, 0))
        if near:
            dj = qi - j
            row = lax.broadcasted_iota(jnp.int32, (T, T), 0)
            col = lax.broadcasted_iota(jnp.int32, (T, T), 1)
            valid = row - col + dj * T >= 0
        for c in range(n_sub):
            pv = jnp.dot(p_scr[c][nslot], vt_prev, preferred_element_type=F32)
            s = s_scr[c][slot]
            s_scr[c][nslot] = scores(c, kt_next)
            if near:
                if bias_ref is not None:
                    s = s + bias_ref[0, dj]
                s = jnp.where(valid, s, NEG)
            m_prev = m_scr[c]
            m_new = jnp.maximum(m_prev, jnp.max(s, axis=-1, keepdims=True))
            alpha = jnp.exp2(m_prev - m_new)
            ps = [jnp.exp2(s[:, i * LANE:(i + 1) * LANE] - m_new) for i in range(T // LANE)]
            p_scr[c][slot] = jnp.concatenate(ps, axis=1).astype(p_scr[c].dtype)
            l_scr[c] = alpha * l_scr[c] + functools.reduce(lambda a, b: a + b, ps)
            acc_scr[c] = alpha * (acc_scr[c] + pv)
            m_scr[c] = m_new

    def far_body(j, _):
        step(j, False)
        return 0

    def near_body(j, _):
        step(j, True)
        return 0

    near_lo = jnp.maximum(qi - n_near + 1, 0)
    lax.fori_loop(0, near_lo, far_body, 0)
    lax.fori_loop(near_lo, qi + 1, near_body, 0)

    last = lax.rem(qi, 2)
    vt = v_tile(qi)
    outs = []
    for c in range(n_sub):
        acc = acc_scr[c] + jnp.dot(p_scr[c][last], vt, preferred_element_type=F32)
        outs.append(acc / jnp.sum(l_scr[c], axis=-1, keepdims=True))
    if mode == "C":
        lp = lam_ref[...]
        lam = (jnp.exp(jnp.sum(lp[0:1] * lp[1:2], axis=-1, keepdims=True))
               - jnp.exp(jnp.sum(lp[2:3] * lp[3:4], axis=-1, keepdims=True)) + lam_init)
        o = outs[0] - lam * outs[1]
        o = o * lax.rsqrt(jnp.mean(o * o, axis=-1, keepdims=True) + RMS_EPS) * g_ref[...]
        o = o * (1.0 - lam_init)
    else:
        o = outs[0]
    o_ref[...] = o.astype(o_ref.dtype)


def flash_pipe(mode, q_arr, k_arr, v_arr, *, S, T, q_slot, k_slot, v_slot, n_units=4, bias=None,
               selb=None, lam_params=None, subln_g=None, lam_init=0.0):
    dq = q_arr.shape[-1]
    dk = k_arr.shape[-1]
    dv = v_arr.shape[-1]
    n_near = 1 if mode == "D" else _n_near(T)
    n_sub = 2 if mode == "C" else 1
    in_specs = [pl.BlockSpec((None, T, dq), lambda u, i: (q_slot(u), i, 0)),
                pl.BlockSpec((None, S, dk), lambda u, i: (k_slot(u), 0, 0)),
                pl.BlockSpec((None, S, dv), lambda u, i: (v_slot(u), 0, 0))]
    args = [q_arr, k_arr, v_arr]
    if bias is not None:
        in_specs.append(pl.BlockSpec((None,) + bias.shape[1:], lambda u, i: (u, 0, 0, 0, 0)))
        args.append(bias)
    if selb is not None:
        in_specs.append(pl.BlockSpec((None, T, selb.shape[-1]), lambda u, i: (u, i, 0)))
        args.append(selb)
    if lam_params is not None:
        in_specs.append(pl.BlockSpec(lam_params.shape, lambda u, i: (0, 0)))
        in_specs.append(pl.BlockSpec(subln_g.shape, lambda u, i: (0, 0)))
        args += [lam_params, subln_g]
    return pl.pallas_call(
        functools.partial(_flash_pipe_kernel, mode=mode, T=T, n_near=n_near, lam_init=lam_init),
        grid=(n_units, S // T),
        in_specs=in_specs,
        out_specs=pl.BlockSpec((None, T, LANE), lambda u, i: (u, i, 0)),
        out_shape=jax.ShapeDtypeStruct((n_units, S, LANE), BF16),
        scratch_shapes=([pltpu.VMEM((n_sub, T, LANE), F32), pltpu.VMEM((n_sub, T, LANE), F32),
                         pltpu.VMEM((n_sub, T, dv), F32)]
                        + [pltpu.VMEM((2, T, T), F32)] * n_sub + [pltpu.VMEM((2, T, T), BF16)] * n_sub),
        compiler_params=_cparams("parallel", "arbitrary"),
        name="flashp_" + mode,
    )(*args)


def flash(mode, q_arr, k_arr, v_arr, *, S, T, q_slot, k_slot, v_slot, n_units=4, bias=None,
          sinks=None, selb=None, lam_params=None, subln_g=None, lam_init=0.0):
    dq = q_arr.shape[-1]
    dk = k_arr.shape[-1]
    dv = v_arr.shape[-1]
    if mode == "B":
        n_near, has_far, window = 2, False, SWA_WINDOW
    elif mode == "D":
        n_near, has_far, window = 1, True, None
    else:
        n_near, has_far, window = _n_near(T), True, None
    in_specs = [pl.BlockSpec((None, T, dq), lambda u, i: (q_slot(u), i, 0)),
                pl.BlockSpec((None, S, dk), lambda u, i: (k_slot(u), 0, 0)),
                pl.BlockSpec((None, S, dv), lambda u, i: (v_slot(u), 0, 0))]
    args = [q_arr, k_arr, v_arr]
    if bias is not None:
        in_specs.append(pl.BlockSpec((None,) + bias.shape[1:], lambda u, i: (u, 0, 0, 0, 0)))
        args.append(bias)
    if sinks is not None:
        in_specs.append(pl.BlockSpec(memory_space=pltpu.SMEM))
        args.append(sinks)
    if selb is not None:
        in_specs.append(pl.BlockSpec((None, T, selb.shape[-1]), lambda u, i: (u, i, 0)))
        args.append(selb)
    if lam_params is not None:
        in_specs.append(pl.BlockSpec(lam_params.shape, lambda u, i: (0, 0)))
        in_specs.append(pl.BlockSpec(subln_g.shape, lambda u, i: (0, 0)))
        args += [lam_params, subln_g]
    return pl.pallas_call(
        functools.partial(_flash_kernel, mode=mode, T=T, n_near=n_near, has_far=has_far,
                          window=window, lam_init=lam_init),
        grid=(n_units, S // T),
        in_specs=in_specs,
        out_specs=pl.BlockSpec((None, T, LANE), lambda u, i: (u, i, 0)),
        out_shape=jax.ShapeDtypeStruct((n_units, S, LANE), BF16),
        compiler_params=_cparams("parallel", "arbitrary"),
        name="flash_" + mode,
    )(*args)


def _rms(x, g):
    return x * lax.rsqrt(jnp.mean(x * x, axis=-1, keepdims=True) + RMS_EPS) * g


def _mla_prep_kernel(hd_ref, gq_ref, gkv_ref, wqa_ref, wqb_ref, wkv_ref, cos_ref, sin_ref,
                     qo_ref, ko_ref, vo_ref, *, scale):
    cosm = cos_ref[...]
    sinm = sin_ref[...]
    cq = jnp.concatenate([hd_ref[c] for c in range(4)], axis=1)
    cqn = _rms(cq, gq_ref[...]).astype(BF16)
    qa = jnp.dot(cqn, wqa_ref[...], preferred_element_type=F32)
    qb = jnp.dot(cqn, wqb_ref[...], preferred_element_type=F32)
    w = 2 * LANE
    for h in range(D_HEADS):
        qe = (qa[:, h * w:(h + 1) * w] * cosm + qb[:, h * w:(h + 1) * w] * sinm) * scale
        qo_ref[h] = qe.astype(qo_ref.dtype)
    ckv = jnp.concatenate([hd_ref[4], hd_ref[5]], axis=1)
    ckvn = _rms(ckv, gkv_ref[...]).astype(BF16)
    kv = jnp.dot(ckvn, wkv_ref[...], preferred_element_type=F32)
    kpe = hd_ref[6] * cosm[:, LANE:] + hd_ref[7] * sinm[:, LANE:]
    for h in range(D_HEADS):
        ko_ref[h] = jnp.concatenate([kv[:, h * LANE:(h + 1) * LANE], kpe], axis=1).astype(ko_ref.dtype)
        vo_ref[h] = kv[:, (D_HEADS + h) * LANE:(D_HEADS + h + 1) * LANE].astype(vo_ref.dtype)


def mla_prep(hd, gq, gkv, wqa, wqb, wkv, cosm, sinm, *, S, tm):
    w = 2 * LANE
    full = lambda a: pl.BlockSpec(a.shape, lambda i: (0,) * a.ndim)
    return pl.pallas_call(
        functools.partial(_mla_prep_kernel, scale=(D_NOPE + D_ROPE) ** -0.5 * LOG2E),
        grid=(S // tm,),
        in_specs=[pl.BlockSpec((8, tm, LANE), lambda i: (0, i, 0)),
                  full(gq), full(gkv), full(wqa), full(wqb), full(wkv),
                  pl.BlockSpec((tm, w), lambda i: (i, 0)),
                  pl.BlockSpec((tm, w), lambda i: (i, 0))],
        out_specs=[pl.BlockSpec((D_HEADS, tm, w), lambda i: (0, i, 0)),
                   pl.BlockSpec((D_HEADS, tm, w), lambda i: (0, i, 0)),
                   pl.BlockSpec((D_HEADS, tm, LANE), lambda i: (0, i, 0))],
        out_shape=[jax.ShapeDtypeStruct((D_HEADS, S, w), BF16),
                   jax.ShapeDtypeStruct((D_HEADS, S, w), BF16),
                   jax.ShapeDtypeStruct((D_HEADS, S, LANE), BF16)],
        compiler_params=_cparams("parallel"),
        name="mla_prep",
    )(hd, gq, gkv, wqa, wqb, wkv, cosm, sinm)


def _merge_kernel(x_ref, oa_ref, ob_ref, oc_ref, od_ref, wg_ref, bg_ref, wbr_ref, o_ref):
    x = x_ref[...]
    acc = None
    for n, o in enumerate((oa_ref, ob_ref, oc_ref, od_ref)):
        g = jnp.dot(x, wg_ref[n], preferred_element_type=F32) + bg_ref[n]
        ocat = jnp.concatenate([o[c] for c in range(4)], axis=1)
        b = jnp.dot(ocat, wbr_ref[n], preferred_element_type=F32)
        t = jax.nn.sigmoid(g) * b
        acc = t if acc is None else acc + t
    o_ref[...] = acc.astype(o_ref.dtype)


def merge(x_bf, outs, wg, bg, wbr, *, tm, tn):
    S, D = x_bf.shape
    o_spec = pl.BlockSpec((4, tm, LANE), lambda j, i: (0, i, 0))
    return pl.pallas_call(
        _merge_kernel,
        grid=(D // tn, S // tm),
        in_specs=[pl.BlockSpec((tm, D), lambda j, i: (i, 0)), o_spec, o_spec, o_spec, o_spec,
                  pl.BlockSpec((N_BRANCHES, D, tn), lambda j, i: (0, 0, j)),
                  pl.BlockSpec((N_BRANCHES, 1, tn), lambda j, i: (0, 0, j)),
                  pl.BlockSpec((N_BRANCHES, BRANCH_W, tn), lambda j, i: (0, 0, j))],
        out_specs=pl.BlockSpec((tm, tn), lambda j, i: (i, j)),
        out_shape=jax.ShapeDtypeStruct((S, D), BF16),
        compiler_params=_cparams("parallel", "parallel"),
        name="merge",
    )(x_bf, *outs, wg, bg, wbr)


def _layer_norm(y, g, b):
    mu = jnp.mean(y, axis=-1, keepdims=True)
    var = jnp.mean(jnp.square(y - mu), axis=-1, keepdims=True)
    return (y - mu) * lax.rsqrt(var + LN_EPS) * g + b


def _out_ln_kernel(x_ref, m_ref, w_ref, g_ref, b_ref, of_ref, ob_ref, *, alpha):
    y = alpha * x_ref[...] + jnp.dot(m_ref[...], w_ref[...], preferred_element_type=F32)
    o = _layer_norm(y, g_ref[...], b_ref[...])
    of_ref[...] = o
    ob_ref[...] = o.astype(ob_ref.dtype)


def out_ln(x, merged, w_out, g, b, *, alpha, tm):
    S, D = x.shape
    row = pl.BlockSpec((tm, D), lambda i: (i, 0))
    vec = pl.BlockSpec((1, D), lambda i: (0, 0))
    return pl.pallas_call(
        functools.partial(_out_ln_kernel, alpha=alpha),
        grid=(S // tm,),
        in_specs=[row, row, pl.BlockSpec((D, D), lambda i: (0, 0)), vec, vec],
        out_specs=[row, row],
        out_shape=[jax.ShapeDtypeStruct((S, D), F32), jax.ShapeDtypeStruct((S, D), BF16)],
        compiler_params=_cparams("parallel"),
        name="out_ln",
    )(x, merged, w_out, g, b)


def _add_ln_kernel(x_ref, y_ref, g_ref, b_ref, of_ref, ob_ref, *, alpha):
    o = _layer_norm(alpha * x_ref[...] + y_ref[...], g_ref[...], b_ref[...])
    of_ref[...] = o
    ob_ref[...] = o.astype(ob_ref.dtype)


def add_ln(x, y, g, b, *, alpha, tm):
    S, D = x.shape
    row = pl.BlockSpec((tm, D), lambda i: (i, 0))
    vec = pl.BlockSpec((1, D), lambda i: (0, 0))
    return pl.pallas_call(
        functools.partial(_add_ln_kernel, alpha=alpha),
        grid=(S // tm,),
        in_specs=[row, row, vec, vec],
        out_specs=[row, row],
        out_shape=[jax.ShapeDtypeStruct((S, D), F32), jax.ShapeDtypeStruct((S, D), BF16)],
        compiler_params=_cparams("parallel"),
        name="add_ln",
    )(x, y, g, b)


def _top_sorted(s, k):
    vals = []
    cur = s
    for r in range(k):
        m = jnp.max(cur, axis=0, keepdims=True)
        vals.append(m)
        if r + 1 < k:
            cur = jnp.where(cur == m, -jnp.inf, cur)
    return vals


def _peer_route_kernel(q_ref, keys_ref, s0_ref, s1_ref, e1_ref, c_ref, tau_ref):
    K = PEER_TOPK

    def head(h, _):
        s0 = lax.dot_general(keys_ref[h, 0], q_ref[2 * h], _NT, precision=lax.Precision.HIGHEST,
                             preferred_element_type=F32)
        s1 = lax.dot_general(keys_ref[h, 1], q_ref[2 * h + 1], _NT, precision=lax.Precision.HIGHEST,
                             preferred_element_type=F32)
        v0 = _top_sorted(s0, K)
        v1 = jnp.concatenate(_top_sorted(s1, K), axis=0)
        hk = K // 2
        cands = ([v0[0] + v1[:hk], v0[0] + v1[hk:]]
                 + [v0[a] + v1[:hk] for a in range(1, hk)]
                 + [jnp.concatenate(v0[hk:], axis=0) + v1[0:1]])
        z = None
        for r in range(K):
            m = functools.reduce(jnp.maximum, cands)
            m = jnp.max(m, axis=0, keepdims=True)
            if r == 0:
                top = m
                z = jnp.ones_like(m)
            else:
                z = z + jnp.exp(m - top)
            if r + 1 < K:
                cands = [jnp.where(c == m, -jnp.inf, c) for c in cands]
        s0_ref[h] = s0
        s1_ref[h] = s1
        e1_ref[h] = jnp.exp(s1 - v1[0:1])
        c_ref[h] = jnp.exp(s0 - v0[0]) / z
        tau_ref[pl.ds(h, 1), :] = m
        return 0

    lax.fori_loop(0, PEER_HEADS, head, 0)


def peer_route(q_slots, sub_keys, *, S, tq):
    arr = jax.ShapeDtypeStruct((PEER_HEADS, PEER_N_KEYS, S), F32)
    spec = pl.BlockSpec((PEER_HEADS, PEER_N_KEYS, tq), lambda i: (0, 0, i))
    return pl.pallas_call(
        _peer_route_kernel,
        grid=(S // tq,),
        in_specs=[pl.BlockSpec((2 * PEER_HEADS, tq, LANE), lambda i: (0, i, 0)),
                  pl.BlockSpec(sub_keys.shape, lambda i: (0, 0, 0, 0))],
        out_specs=[spec, spec, spec, spec, pl.BlockSpec((PEER_HEADS, tq), lambda i: (0, i))],
        out_shape=[arr, arr, arr, arr, jax.ShapeDtypeStruct((PEER_HEADS, S), F32)],
        compiler_params=_cparams("parallel"),
        name="peer_route",
    )(q_slots, sub_keys)


def _peer_expert_kernel(x_ref, u_ref, v_ref, s0a_ref, s0b_ref, ca_ref, cb_ref, s1_ref, e1_ref, tau_ref,
                        y_ref, ha_scr, hb_scr, wa_scr, wb_scr):
    k = pl.program_id(1)
    te, tq = ha_scr.shape
    n_rows = te // PEER_N_KEYS
    D = y_ref.shape[1]
    nc = D // n_rows

    @pl.when(k == 0)
    def _():
        y_ref[...] = jnp.zeros_like(y_ref)
        hb_scr[...] = jnp.zeros_like(hb_scr)
        wa_scr[...] = jnp.zeros_like(wa_scr)

    def phase(u_lo, h_out, w_prev, h_prev, w_out, s0_ref, c_ref):
        rb = 32

        def y_chunk(r):
            cols = pl.ds(r * nc, nc)
            y_ref[:, cols] += lax.dot_general(w_prev[...], v_ref[pl.ds(u_lo, te), cols], _TN,
                                              preferred_element_type=F32)

        def h_half(n):
            toks = pl.ds(n * (tq // 2), tq // 2)
            h_out[:, toks] = lax.dot_general(u_ref[pl.ds(u_lo, te), :], x_ref[toks, :], _NT,
                                             preferred_element_type=F32)

        def gate_block(tc, sb):
            tl = pl.ds(tc * LANE, LANE)
            keys = pl.ds(sb * rb, rb)
            gs = [None] * n_rows
            for h in range(PEER_HEADS):
                s1b = s1_ref[h, keys, tl]
                e1b = e1_ref[h, keys, tl]
                taub = tau_ref[pl.ds(h, 1), tl]
                for r in range(n_rows):
                    ssum = s0_ref[h, pl.ds(r, 1), tl] + s1b
                    t = jnp.where(ssum >= taub, e1b, 0.0) * c_ref[h, pl.ds(r, 1), tl]
                    gs[r] = t if gs[r] is None else gs[r] + t
            for r in range(n_rows):
                rows = pl.ds(r * PEER_N_KEYS + sb * rb, rb)
                hp = h_prev[rows, tl]
                act = 0.5 * hp * (1.0 + lax.erf(hp * math.sqrt(0.5)))
                w_out[rows, tl] = (gs[r] * act).astype(w_out.dtype)

        blocks = [(tc, sb) for tc in range(tq // LANE) for sb in range(PEER_N_KEYS // rb)]
        pieces = [functools.partial(y_chunk, r) for r in range(n_rows)] + [functools.partial(h_half, n) for n in range(2)]
        weights = [1] * n_rows + [2, 2]
        per = len(blocks) // sum(weights)
        b0 = 0
        for piece, wgt in zip(pieces, weights):
            piece()
            for tc, sb in blocks[b0:b0 + per * wgt]:
                gate_block(tc, sb)
            b0 += per * wgt
        for tc, sb in blocks[b0:]:
            gate_block(tc, sb)

    phase(0, ha_scr, wa_scr, hb_scr, wb_scr, s0a_ref, ca_ref)
    phase(te, hb_scr, wb_scr, ha_scr, wa_scr, s0b_ref, cb_ref)


def peer_experts(x_bf, u_bf, v_bf, s0, s1, e1, c, tau, *, tq, te):
    S, D = x_bf.shape
    E = u_bf.shape[0]
    n_tiles = E // te
    n_blocks = n_tiles // 2
    n_rows = te // PEER_N_KEYS
    s0_t = s0.reshape(PEER_HEADS, n_tiles, n_rows, S)
    c_t = c.reshape(PEER_HEADS, n_tiles, n_rows, S)
    rt = pl.BlockSpec((PEER_HEADS, PEER_N_KEYS, tq), lambda i, k: (0, 0, i))
    row_a = pl.BlockSpec((PEER_HEADS, None, n_rows, tq), lambda i, k: (0, jnp.maximum(2 * k - 1, 0), 0, i))
    row_b = pl.BlockSpec((PEER_HEADS, None, n_rows, tq),
                         lambda i, k: (0, jnp.minimum(2 * k, n_tiles - 1), 0, i))
    return pl.pallas_call(
        _peer_expert_kernel,
        grid=(S // tq, n_blocks + 1),
        in_specs=[pl.BlockSpec((tq, D), lambda i, k: (i, 0)),
                  pl.BlockSpec((2 * te, D), lambda i, k: (jnp.minimum(k, n_blocks - 1), 0)),
                  pl.BlockSpec((2 * te, D), lambda i, k: (jnp.maximum(k - 1, 0), 0)),
                  row_a, row_b, row_a, row_b, rt, rt,
                  pl.BlockSpec((PEER_HEADS, tq), lambda i, k: (0, i))],
        out_specs=pl.BlockSpec((tq, D), lambda i, k: (i, 0)),
        out_shape=jax.ShapeDtypeStruct((S, D), F32),
        scratch_shapes=[pltpu.VMEM((te, tq), F32), pltpu.VMEM((te, tq), F32),
                        pltpu.VMEM((te, tq), BF16), pltpu.VMEM((te, tq), BF16)],
        compiler_params=_cparams("parallel", "arbitrary"),
        name="peer_experts",
    )(x_bf, u_bf, v_bf, s0_t, s0_t, c_t, c_t, s1, e1, tau)


def _pad_cols(w, width):
    return jnp.pad(w, ((0, 0), (0, width - w.shape[1])))


def _rot_cols(w):
    h = w.shape[1] // 2
    return jnp.concatenate([-w[:, h:], w[:, :h]], axis=1)


def _split_w_in(w):
    widths = (512, 512, 512, 512, 128, 128, 512, 512, 512, D_Q_LORA, D_KV_LORA, D_ROPE)
    out, o = [], 0
    for n in widths:
        out.append(w[:, o:o + n])
        o += n
    return out


def _layer_weights(w_in, a_scale, b_scale, c_scale):
    (a_q, a_k, a_v, b_q, b_k, b_v, c_q, c_k, c_v, d_cq, d_ckv, d_kr) = _split_w_in(w_in)
    b_k0, b_k1 = b_k[:, :64], b_k[:, 64:]
    b_v0, b_v1 = b_v[:, :64], b_v[:, 64:]
    w_abc = jnp.concatenate(
        [a_q, a_k, a_v,
         b_q, _pad_cols(b_k0, LANE), _pad_cols(b_k1, LANE), _pad_cols(b_v0, LANE), _pad_cols(b_v1, LANE),
         c_q, c_k, c_v], axis=1).astype(BF16)
    ones = lambda n: jnp.ones((n,), F32)
    s_abc = jnp.concatenate([ones(512) * a_scale, ones(1024),
                             ones(512) * b_scale, ones(512),
                             ones(512) * c_scale, ones(1024)])[None, :]
    w_d = jnp.concatenate([d_cq, d_ckv, _pad_cols(d_kr, LANE), _pad_cols(_rot_cols(d_kr), LANE)],
                          axis=1).astype(BF16)
    return w_abc, s_abc, w_d


def _mla_weights(w_uq, w_ukv):
    wq = w_uq.reshape(D_Q_LORA, D_HEADS, D_NOPE + D_ROPE)
    nope, pe = wq[..., :D_NOPE], wq[..., D_NOPE:]
    pe_rot = jnp.concatenate([-pe[..., D_ROPE // 2:], pe[..., :D_ROPE // 2]], axis=-1)
    z64 = jnp.zeros((D_Q_LORA, D_HEADS, LANE - D_ROPE), F32)
    z128 = jnp.zeros((D_Q_LORA, D_HEADS, D_NOPE), F32)
    wqa = jnp.concatenate([nope, pe, z64], axis=-1).reshape(D_Q_LORA, -1).astype(BF16)
    wqb = jnp.concatenate([z128, pe_rot, z64], axis=-1).reshape(D_Q_LORA, -1).astype(BF16)
    wkv = w_ukv.reshape(D_KV_LORA, D_HEADS, D_NOPE + D_V)
    wkv = jnp.concatenate([wkv[..., :D_NOPE].reshape(D_KV_LORA, -1),
                           wkv[..., D_NOPE:].reshape(D_KV_LORA, -1)], axis=1).astype(BF16)
    return wqa, wqb, wkv


def _rope_tables(positions):
    half = D_ROPE // 2
    inv = ROPE_THETA ** (-jnp.arange(half, dtype=F32) / half)
    ang = positions.astype(F32)[..., None] * inv
    cos, sin = jnp.cos(ang), jnp.sin(ang)
    S = positions.shape[0]
    one = jnp.ones((S, D_NOPE), F32)
    z = jnp.zeros((S, LANE - D_ROPE), F32)
    cosm = jnp.concatenate([one, cos, cos, z], axis=1)
    sinm = jnp.concatenate([jnp.zeros((S, D_NOPE), F32), sin, sin, z], axis=1)
    return cosm, sinm


def kernel(x, positions, t5_table, w_in, w_gate, b_gate, w_branch, w_out, swa_sinks, diff_lambda_q1, diff_lambda_k1, diff_lambda_q2, diff_lambda_k2, diff_subln_g, mla_q_norm_g, mla_kv_norm_g, mla_w_uq, mla_w_ukv, ln1_g, ln1_b, peer_w_q, peer_sub_keys, peer_u, peer_v, ln2_g, ln2_b):
    B, S, D = x.shape
    assert B == 1 and D == D_MODEL and S % MOBA_BLOCK == 0
    depth = w_in.shape[0]
    alpha = (2.0 * depth) ** 0.25
    tm = min(512, S)
    T = min(512, S)
    TB = SWA_WINDOW

    tab_a = t5_table[:, :A_HEADS]
    tab_b = t5_table[:, A_HEADS:A_HEADS + B_HEADS]
    tab_c = t5_table[:, A_HEADS + B_HEADS:]
    bias_a = t5_tiles(tab_a, T=T, n_near=_n_near(T))[:, None]
    bias_c = t5_tiles(tab_c, T=T, n_near=_n_near(T))[:, None]
    bias_b = t5_tiles(tab_b, T=TB, n_near=2).reshape(B_HEADS // 2, 2, 2, TB, TB)
    cosm, sinm = _rope_tables(positions[0])

    xf = x[0]
    xb = xf.astype(BF16)
    for layer in range(depth):
        lam_init = 0.8 - 0.6 * math.exp(-0.3 * layer)
        w_abc, s_abc, w_d = _layer_weights(w_in[layer], 128 ** -0.5 * LOG2E, B_HEAD_DIM ** -0.5 * LOG2E,
                                           C_HEAD_DIM ** -0.5 * LOG2E)
        hs = proj_slots(xb, w_abc, s_abc, BF16, tm=tm, slots_per_step=8)
        hd = proj_slots(xb, w_d, jnp.ones((1, w_d.shape[1]), F32), F32, tm=tm, slots_per_step=8)

        nb = S // MOBA_BLOCK
        nbp = -(-nb // LANE) * LANE
        kmean = jnp.pad(moba_kmean(hs, 4, S), ((0, 0), (0, nbp - nb), (0, 0)))
        selb = moba_select(hs, 0, kmean, S)
        o_a = flash_pipe("A", hs, hs, hs, S=S, T=T, q_slot=lambda u: u, k_slot=lambda u: 4 + u,
                    v_slot=lambda u: 8 + u, bias=bias_a, selb=selb)
        sinks = ((swa_sinks[layer] - tab_b[T5_BUCKETS - 1]) * LOG2E).reshape(B_HEADS // 2, 2)
        o_b = flash("B", hs, hs, hs, S=S, T=TB, q_slot=lambda u: 12 + u, k_slot=lambda u: 16 + u // 2,
                    v_slot=lambda u: 18 + u // 2, bias=bias_b, sinks=sinks)
        lam_params = jnp.stack([diff_lambda_q1[layer], diff_lambda_k1[layer],
                                diff_lambda_q2[layer], diff_lambda_k2[layer]])
        o_c = flash_pipe("C", hs, hs, hs, S=S, T=T, q_slot=lambda u: 20 + u, k_slot=lambda u: 24 + u,
                    v_slot=lambda u: 28 + u, bias=bias_c, lam_params=lam_params,
                    subln_g=diff_subln_g[layer][None, :], lam_init=lam_init)
        wqa, wqb, wkv = _mla_weights(mla_w_uq[layer], mla_w_ukv[layer])
        q_d, k_d, v_d = mla_prep(hd, mla_q_norm_g[layer][None, :], mla_kv_norm_g[layer][None, :],
                                 wqa, wqb, wkv, cosm, sinm, S=S, tm=tm)
        o_d = flash_pipe("D", q_d, k_d, v_d, S=S, T=T, q_slot=lambda u: u, k_slot=lambda u: u,
                    v_slot=lambda u: u)

        merged = merge(xb, (o_a, o_b, o_c, o_d),
                       w_gate[layer].astype(BF16), b_gate[layer][:, None, :], w_branch[layer].astype(BF16), tm=tm, tn=512)
        xf, xb = out_ln(xf, merged, w_out[layer].astype(BF16), ln1_g[layer][None, :],
                        ln1_b[layer][None, :], alpha=alpha, tm=tm)

        q_slots = proj_slots(xb, peer_w_q[layer].astype(BF16), jnp.ones((1, D), F32), F32,
                             tm=tm, slots_per_step=8)
        s0, s1, e1, c, tau = peer_route(q_slots, peer_sub_keys[layer], S=S, tq=tm)
        y = peer_experts(xb, peer_u[layer].astype(BF16), peer_v[layer].astype(BF16),
                         s0, s1, e1, c, tau, tq=tm, te=512)
        xf, xb = add_ln(xf, y, ln2_g[layer][None, :], ln2_b[layer][None, :], alpha=alpha, tm=tm)
    return xf[None]
```

```python
import functools
import math

import jax
import jax.numpy as jnp
from jax import lax
from jax.experimental import pallas as pl
from jax.experimental.pallas import tpu as pltpu

F32 = jnp.float32
BF16 = jnp.bfloat16
LANE = 128
NEG = -1e30
LOG2E = math.log2(math.e)

D_MODEL = 2048
A_HEADS = 4
MOBA_BLOCK = 256
MOBA_TOPK = 3
B_HEADS = 8
B_KV_HEADS = 2
B_HEAD_DIM = 64
SWA_WINDOW = 128
C_HEADS = 4
C_HEAD_DIM = 64
D_HEADS = 4
D_Q_LORA = 512
D_KV_LORA = 256
D_NOPE = 128
D_ROPE = 64
D_V = 128
ROPE_THETA = 10000.0
N_BRANCHES = 4
BRANCH_W = 512
T5_BUCKETS = 32
T5_MAX_EXACT = T5_BUCKETS // 2
T5_MAX_DISTANCE = 1024
PEER_HEADS = 8
PEER_N_KEYS = 128
PEER_D_KEY = 128
PEER_TOPK = 16
LN_EPS = 1e-5
RMS_EPS = 1e-6

VMEM_LIMIT_MB = 56


def _cparams(*sem):
    return pltpu.CompilerParams(dimension_semantics=sem,
                                vmem_limit_bytes=VMEM_LIMIT_MB * 1024 * 1024)


_NT = (((1,), (1,)), ((), ()))
_TN = (((0,), (0,)), ((), ()))


def _proj_kernel(x_ref, w_ref, s_ref, o_ref):
    acc = jnp.dot(x_ref[...], w_ref[...], preferred_element_type=F32) * s_ref[...]
    for c in range(o_ref.shape[0]):
        o_ref[c] = acc[:, c * LANE:(c + 1) * LANE].astype(o_ref.dtype)


def proj_slots(x_bf, w_bf, colscale, out_dtype, *, tm, slots_per_step):
    S, K = x_bf.shape
    N = w_bf.shape[1]
    tn = slots_per_step * LANE
    return pl.pallas_call(
        _proj_kernel,
        grid=(N // tn, S // tm),
        in_specs=[pl.BlockSpec((tm, K), lambda j, i: (i, 0)),
                  pl.BlockSpec((K, tn), lambda j, i: (0, j)),
                  pl.BlockSpec((1, tn), lambda j, i: (0, j))],
        out_specs=pl.BlockSpec((slots_per_step, tm, LANE), lambda j, i: (j, i, 0)),
        out_shape=jax.ShapeDtypeStruct((N // LANE, S, LANE), out_dtype),
        compiler_params=_cparams("parallel", "parallel"),
        name="proj_slots",
    )(x_bf, w_bf, colscale)


def _t5_tiles_kernel(tab_ref, o_ref, *, T, n_heads):
    dj = pl.program_id(0)
    row = lax.broadcasted_iota(jnp.int32, (T, T), 0)
    col = lax.broadcasted_iota(jnp.int32, (T, T), 1)
    bucket = _t5_bucket(jnp.maximum(row - col + dj * T, 0))
    for h in range(n_heads):
        last = tab_ref[T5_BUCKETS - 1, h]
        acc = jnp.zeros((T, T), F32)
        for b in range(T5_BUCKETS - 1):
            acc = jnp.where(bucket == b, (tab_ref[b, h] - last) * LOG2E, acc)
        o_ref[h, 0] = acc


def t5_tiles(tab, *, T, n_near):
    H = tab.shape[1]
    return pl.pallas_call(
        functools.partial(_t5_tiles_kernel, T=T, n_heads=H),
        grid=(n_near,),
        in_specs=[pl.BlockSpec(memory_space=pltpu.SMEM)],
        out_specs=pl.BlockSpec((H, 1, T, T), lambda d: (0, d, 0, 0)),
        out_shape=jax.ShapeDtypeStruct((H, n_near, T, T), F32),
        compiler_params=_cparams("parallel"),
        name="t5_tiles",
    )(tab)


def _n_near(T):
    return (T5_MAX_DISTANCE - 1 + T - 1) // T + 1


def _t5_bucket(n):
    nf = jnp.maximum(n, 1).astype(F32)
    large = T5_MAX_EXACT + (jnp.log(nf / T5_MAX_EXACT) / math.log(T5_MAX_DISTANCE / T5_MAX_EXACT)
                            * (T5_BUCKETS - T5_MAX_EXACT)).astype(jnp.int32)
    return jnp.where(n < T5_MAX_EXACT, n, jnp.minimum(large, T5_BUCKETS - 1))


def _t5_band_kernel(tab_ref, o_ref, *, window):
    h = pl.program_id(0)
    rows, cols = o_ref.shape
    dist = (lax.broadcasted_iota(jnp.int32, (rows, cols), 0)
            - lax.broadcasted_iota(jnp.int32, (rows, cols), 1) + window)
    bucket = _t5_bucket(jnp.maximum(dist, 0))
    last = tab_ref[T5_BUCKETS - 1, h]
    acc = jnp.zeros((rows, cols), F32)
    for b in range(T5_BUCKETS - 1):
        acc = jnp.where(bucket == b, (tab_ref[b, h] - last) * LOG2E, acc)
    o_ref[...] = jnp.where((dist >= 0) & (dist < window), acc, NEG)


def t5_band(tab, *, tq, window):
    H = tab.shape[1]
    cols = tq + 2 * window
    return pl.pallas_call(
        functools.partial(_t5_band_kernel, window=window),
        grid=(H,),
        in_specs=[pl.BlockSpec(memory_space=pltpu.SMEM)],
        out_specs=pl.BlockSpec((None, tq, cols), lambda h: (h, 0, 0)),
        out_shape=jax.ShapeDtypeStruct((H, tq, cols), F32),
        compiler_params=_cparams("parallel"),
        name="t5_band",
    )(tab)


def _swa_kernel(q_ref, k_ref, v_ref, band_ref, sink_ref, o_ref, *, window):
    u = pl.program_id(0)
    qi = pl.program_id(1)
    tq = q_ref.shape[0]
    nk = tq + window
    half = LANE // 2
    start = pl.multiple_of(jnp.maximum(qi * tq - window, 0), window)
    kb = k_ref[pl.ds(start, nk), :]
    vb = v_ref[pl.ds(start, nk), :]
    q = q_ref[...]
    z = jnp.zeros((tq, half), q.dtype)
    outs = []
    for c in range(2):
        qc = jnp.concatenate([q[:, c * half:(c + 1) * half], z], axis=1)
        band = jnp.where(qi == 0, band_ref[c, :, window:], band_ref[c, :, :nk])
        s = lax.dot_general(qc, kb, _NT, preferred_element_type=F32) + band
        sink = sink_ref[u, c]
        m = jnp.maximum(jnp.max(s, axis=-1, keepdims=True), sink)
        p = jnp.exp2(s - m)
        l = jnp.sum(p, axis=-1, keepdims=True) + jnp.exp2(sink - m)
        o = jnp.dot(p.astype(vb.dtype), vb, preferred_element_type=F32) / l
        outs.append(o[:, :half])
    o_ref[...] = jnp.concatenate(outs, axis=1).astype(o_ref.dtype)


def swa_attention(hs, band, sinks, *, S, tq, q_slot0, k_slot0, v_slot0):
    n_units = B_HEADS // 2
    per_kv = n_units // B_KV_HEADS
    cols = band.shape[-1]
    return pl.pallas_call(
        functools.partial(_swa_kernel, window=SWA_WINDOW),
        grid=(n_units, S // tq),
        in_specs=[pl.BlockSpec((None, tq, LANE), lambda u, i: (q_slot0 + u, i, 0)),
                  pl.BlockSpec((None, S, LANE), lambda u, i: (k_slot0 + u // per_kv, 0, 0)),
                  pl.BlockSpec((None, S, LANE), lambda u, i: (v_slot0 + u // per_kv, 0, 0)),
                  pl.BlockSpec((2, tq, cols), lambda u, i: (u, 0, 0)),
                  pl.BlockSpec(memory_space=pltpu.SMEM)],
        out_specs=pl.BlockSpec((None, tq, LANE), lambda u, i: (u, i, 0)),
        out_shape=jax.ShapeDtypeStruct((n_units, S, LANE), BF16),
        compiler_params=_cparams("parallel", "arbitrary"),
        name="swa",
    )(hs, hs, hs, band, sinks)


def _kmean_kernel(k_ref, o_ref, *, blk):
    k = k_ref[...].astype(F32)
    nb = k.shape[0] // blk
    o_ref[...] = k.reshape(nb, blk, LANE).sum(axis=1) * (1.0 / blk)


def moba_kmean(hs, k_slot0, S):
    nb = S // MOBA_BLOCK
    return pl.pallas_call(
        functools.partial(_kmean_kernel, blk=MOBA_BLOCK),
        grid=(A_HEADS,),
        in_specs=[pl.BlockSpec((None, S, LANE), lambda h: (k_slot0 + h, 0, 0))],
        out_specs=pl.BlockSpec((None, nb, LANE), lambda h: (h, 0, 0)),
        out_shape=jax.ShapeDtypeStruct((A_HEADS, nb, LANE), F32),
        compiler_params=_cparams("parallel"),
        name="moba_kmean",
    )(hs)


def _moba_select_kernel(q_ref, km_ref, o_ref, *, topk):
    tq = q_ref.shape[0]
    q = q_ref[...].astype(F32)
    gate = lax.dot_general(q, km_ref[...], _NT, precision=lax.Precision.HIGHEST,
                           preferred_element_type=F32)
    lane = lax.broadcasted_iota(jnp.int32, gate.shape, 1)
    row = lax.broadcasted_iota(jnp.int32, gate.shape, 0)
    own = pl.program_id(1) * (tq // MOBA_BLOCK) + row // MOBA_BLOCK
    nbp = gate.shape[1]
    gate = jnp.where(lane < own, gate, -jnp.inf)
    mask = jnp.where(lane == own, 0.0, NEG)
    for _ in range(topk):
        mx = jnp.max(gate, axis=-1, keepdims=True)
        idx = jnp.min(jnp.where(gate == mx, lane, nbp), axis=-1, keepdims=True)
        hit = lane == idx
        mask = jnp.where(hit, jnp.where(mx > -jnp.inf, 0.0, mask), mask)
        gate = jnp.where(hit, -jnp.inf, gate)
    o_ref[...] = mask.astype(o_ref.dtype)


def moba_select(hs, q_slot0, kmean_pad, S, *, tq):
    nbp = kmean_pad.shape[1]
    return pl.pallas_call(
        functools.partial(_moba_select_kernel, topk=MOBA_TOPK),
        grid=(A_HEADS, S // tq),
        in_specs=[pl.BlockSpec((None, tq, LANE), lambda h, i: (q_slot0 + h, i, 0)),
                  pl.BlockSpec((None, nbp, LANE), lambda h, i: (h, 0, 0))],
        out_specs=pl.BlockSpec((None, tq, nbp), lambda h, i: (h, i, 0)),
        out_shape=jax.ShapeDtypeStruct((A_HEADS, S, nbp), BF16),
        compiler_params=_cparams("parallel", "parallel"),
        name="moba_select",
    )(hs, kmean_pad)


def _flash_pipe_kernel(*refs, mode, T, n_near, lam_init):
    it = iter(refs)
    q_ref, k_ref, v_ref = next(it), next(it), next(it)
    bias_ref = next(it) if mode in "AC" else None
    selb_ref = next(it) if mode == "A" else None
    lam_ref, g_ref = (next(it), next(it)) if mode == "C" else (None, None)
    o_ref = next(it)
    m_scr, l_scr, acc_scr = next(it), next(it), next(it)
    rest = list(it)
    s_scr, p_scr = rest[:len(rest) // 2], rest[len(rest) // 2:]

    qi = pl.program_id(1)
    q = q_ref[...]
    half = LANE // 2
    if mode == "A":
        qs = [jnp.concatenate([q, selb_ref[...]], axis=1)]
    elif mode == "C":
        lane = lax.broadcasted_iota(jnp.int32, q.shape, 1)
        zq = jnp.zeros_like(q)
        qs = [jnp.where(lane < half, q, zq), jnp.where(lane >= half, q, zq)]
    else:
        qs = [q]
    n_sub = len(qs)

    def k_tile(j):
        kt = k_ref[pl.ds(pl.multiple_of(j * T, T), T), :]
        if mode == "A":
            n_blk = selb_ref.shape[-1]
            blk = lax.broadcasted_iota(jnp.int32, (T, n_blk), 1)
            row = lax.broadcasted_iota(jnp.int32, (T, n_blk), 0)
            own = j * (T // MOBA_BLOCK) + row // MOBA_BLOCK
            kt = jnp.concatenate([kt, jnp.where(blk == own, 1.0, 0.0).astype(kt.dtype)], axis=1)
        return kt

    def v_tile(j):
        return v_ref[pl.ds(pl.multiple_of(j * T, T), T), :]

    def scores(c, kt):
        return lax.dot_general(qs[c], kt, _NT, preferred_element_type=F32)

    kt0 = k_tile(0)
    for c in range(n_sub):
        s_scr[c][0] = scores(c, kt0)
        p_scr[c][1] = jnp.zeros((T, T), p_scr[c].dtype)
        m_scr[c] = jnp.full((T, LANE), NEG, F32)
        l_scr[c] = jnp.zeros((T, LANE), F32)
        acc_scr[c] = jnp.zeros(acc_scr.shape[1:], F32)

    def step(j, near):
        slot = lax.rem(j, 2)
        nslot = 1 - slot
        kt_next = k_tile(jnp.minimum(j + 1, qi))
        vt_prev = v_tile(jnp.maximum(j - 1, 0))
        if near:
            dj = qi - j
            row = lax.broadcasted_iota(jnp.int32, (T, T), 0)
            col = lax.broadcasted_iota(jnp.int32, (T, T), 1)
            valid = row - col + dj * T >= 0
        for c in range(n_sub):
            pv = jnp.dot(p_scr[c][nslot], vt_prev, preferred_element_type=F32)
            s = s_scr[c][slot]
            s_scr[c][nslot] = scores(c, kt_next)
            if near:
                if bias_ref is not None:
                    s = s + bias_ref[0, dj]
                s = jnp.where(valid, s, NEG)
            m_prev = m_scr[c]
            m_new = jnp.maximum(m_prev, jnp.max(s, axis=-1, keepdims=True))
            alpha = jnp.exp2(m_prev - m_new)
            ps = [jnp.exp2(s[:, i * LANE:(i + 1) * LANE] - m_new) for i in range(T // LANE)]
            p_scr[c][slot] = jnp.concatenate(ps, axis=1).astype(p_scr[c].dtype)
            l_scr[c] = alpha * l_scr[c] + functools.reduce(lambda a, b: a + b, ps)
            acc_scr[c] = alpha * (acc_scr[c] + pv)
            m_scr[c] = m_new

    def far_body(j, _):
        step(j, False)
        return 0

    def near_body(j, _):
        step(j, True)
        return 0

    near_lo = jnp.maximum(qi - n_near + 1, 0)
    lax.fori_loop(0, near_lo, far_body, 0)
    lax.fori_loop(near_lo, qi + 1, near_body, 0)

    last = lax.rem(qi, 2)
    vt = v_tile(qi)
    outs = []
    for c in range(n_sub):
        acc = acc_scr[c] + jnp.dot(p_scr[c][last], vt, preferred_element_type=F32)
        outs.append(acc / jnp.sum(l_scr[c], axis=-1, keepdims=True))
    if mode == "C":
        lp = lam_ref[...]
        lam = (jnp.exp(jnp.sum(lp[0:1] * lp[1:2], axis=-1, keepdims=True))
               - jnp.exp(jnp.sum(lp[2:3] * lp[3:4], axis=-1, keepdims=True)) + lam_init)
        o = outs[0] - lam * outs[1]
        o = o * lax.rsqrt(jnp.mean(o * o, axis=-1, keepdims=True) + RMS_EPS) * g_ref[...]
        o = o * (1.0 - lam_init)
    else:
        o = outs[0]
    o_ref[...] = o.astype(o_ref.dtype)


def flash_pipe(mode, q_arr, k_arr, v_arr, *, S, T, q_slot, k_slot, v_slot, n_units=4, bias=None,
               selb=None, lam_params=None, subln_g=None, lam_init=0.0):
    dq = q_arr.shape[-1]
    dk = k_arr.shape[-1]
    dv = v_arr.shape[-1]
    n_near = 1 if mode == "D" else _n_near(T)
    n_sub = 2 if mode == "C" else 1
    in_specs = [pl.BlockSpec((None, T, dq), lambda u, i: (q_slot(u), i, 0)),
                pl.BlockSpec((None, S, dk), lambda u, i: (k_slot(u), 0, 0)),
                pl.BlockSpec((None, S, dv), lambda u, i: (v_slot(u), 0, 0))]
    args = [q_arr, k_arr, v_arr]
    if bias is not None:
        in_specs.append(pl.BlockSpec((None,) + bias.shape[1:], lambda u, i: (u, 0, 0, 0, 0)))
        args.append(bias)
    if selb is not None:
        in_specs.append(pl.BlockSpec((None, T, selb.shape[-1]), lambda u, i: (u, i, 0)))
        args.append(selb)
    if lam_params is not None:
        in_specs.append(pl.BlockSpec(lam_params.shape, lambda u, i: (0, 0)))
        in_specs.append(pl.BlockSpec(subln_g.shape, lambda u, i: (0, 0)))
        args += [lam_params, subln_g]
    return pl.pallas_call(
        functools.partial(_flash_pipe_kernel, mode=mode, T=T, n_near=n_near, lam_init=lam_init),
        grid=(n_units, S // T),
        in_specs=in_specs,
        out_specs=pl.BlockSpec((None, T, LANE), lambda u, i: (u, i, 0)),
        out_shape=jax.ShapeDtypeStruct((n_units, S, LANE), BF16),
        scratch_shapes=([pltpu.VMEM((n_sub, T, LANE), F32), pltpu.VMEM((n_sub, T, LANE), F32),
                         pltpu.VMEM((n_sub, T, dv), F32)]
                        + [pltpu.VMEM((2, T, T), F32)] * n_sub + [pltpu.VMEM((2, T, T), BF16)] * n_sub),
        compiler_params=_cparams("parallel", "arbitrary"),
        name="flashp_" + mode,
    )(*args)


def _rms(x, g):
    return x * lax.rsqrt(jnp.mean(x * x, axis=-1, keepdims=True) + RMS_EPS) * g


def _mla_prep_kernel(hd_ref, gq_ref, gkv_ref, wqa_ref, wqb_ref, wkv_ref, cos_ref, sin_ref,
                     qo_ref, ko_ref, vo_ref, *, scale):
    cosm = cos_ref[...]
    sinm = sin_ref[...]
    cq = jnp.concatenate([hd_ref[c] for c in range(4)], axis=1)
    cqn = _rms(cq, gq_ref[...]).astype(BF16)
    qa = jnp.dot(cqn, wqa_ref[...], preferred_element_type=F32)
    qb = jnp.dot(cqn, wqb_ref[...], preferred_element_type=F32)
    w = 2 * LANE
    for h in range(D_HEADS):
        qe = (qa[:, h * w:(h + 1) * w] * cosm + qb[:, h * w:(h + 1) * w] * sinm) * scale
        qo_ref[h] = qe.astype(qo_ref.dtype)
    ckv = jnp.concatenate([hd_ref[4], hd_ref[5]], axis=1)
    ckvn = _rms(ckv, gkv_ref[...]).astype(BF16)
    kv = jnp.dot(ckvn, wkv_ref[...], preferred_element_type=F32)
    kpe = hd_ref[6] * cosm[:, LANE:] + hd_ref[7] * sinm[:, LANE:]
    for h in range(D_HEADS):
        ko_ref[h] = jnp.concatenate([kv[:, h * LANE:(h + 1) * LANE], kpe], axis=1).astype(ko_ref.dtype)
        vo_ref[h] = kv[:, (D_HEADS + h) * LANE:(D_HEADS + h + 1) * LANE].astype(vo_ref.dtype)


def mla_prep(hd, gq, gkv, wqa, wqb, wkv, cosm, sinm, *, S, tm):
    w = 2 * LANE
    full = lambda a: pl.BlockSpec(a.shape, lambda i: (0,) * a.ndim)
    return pl.pallas_call(
        functools.partial(_mla_prep_kernel, scale=(D_NOPE + D_ROPE) ** -0.5 * LOG2E),
        grid=(S // tm,),
        in_specs=[pl.BlockSpec((8, tm, LANE), lambda i: (0, i, 0)),
                  full(gq), full(gkv), full(wqa), full(wqb), full(wkv),
                  pl.BlockSpec((tm, w), lambda i: (i, 0)),
                  pl.BlockSpec((tm, w), lambda i: (i, 0))],
        out_specs=[pl.BlockSpec((D_HEADS, tm, w), lambda i: (0, i, 0)),
                   pl.BlockSpec((D_HEADS, tm, w), lambda i: (0, i, 0)),
                   pl.BlockSpec((D_HEADS, tm, LANE), lambda i: (0, i, 0))],
        out_shape=[jax.ShapeDtypeStruct((D_HEADS, S, w), BF16),
                   jax.ShapeDtypeStruct((D_HEADS, S, w), BF16),
                   jax.ShapeDtypeStruct((D_HEADS, S, LANE), BF16)],
        compiler_params=_cparams("parallel"),
        name="mla_prep",
    )(hd, gq, gkv, wqa, wqb, wkv, cosm, sinm)


def _merge_kernel(x_ref, oa_ref, ob_ref, oc_ref, od_ref, wg_ref, bg_ref, wbr_ref, o_ref):
    x = x_ref[...]
    acc = None
    for n, o in enumerate((oa_ref, ob_ref, oc_ref, od_ref)):
        g = jnp.dot(x, wg_ref[n], preferred_element_type=F32) + bg_ref[n]
        ocat = jnp.concatenate([o[c] for c in range(4)], axis=1)
        b = jnp.dot(ocat, wbr_ref[n], preferred_element_type=F32)
        t = jax.nn.sigmoid(g) * b
        acc = t if acc is None else acc + t
    o_ref[...] = acc.astype(o_ref.dtype)


def merge(x_bf, outs, wg, bg, wbr, *, tm, tn):
    S, D = x_bf.shape
    o_spec = pl.BlockSpec((4, tm, LANE), lambda j, i: (0, i, 0))
    return pl.pallas_call(
        _merge_kernel,
        grid=(D // tn, S // tm),
        in_specs=[pl.BlockSpec((tm, D), lambda j, i: (i, 0)), o_spec, o_spec, o_spec, o_spec,
                  pl.BlockSpec((N_BRANCHES, D, tn), lambda j, i: (0, 0, j)),
                  pl.BlockSpec((N_BRANCHES, 1, tn), lambda j, i: (0, 0, j)),
                  pl.BlockSpec((N_BRANCHES, BRANCH_W, tn), lambda j, i: (0, 0, j))],
        out_specs=pl.BlockSpec((tm, tn), lambda j, i: (i, j)),
        out_shape=jax.ShapeDtypeStruct((S, D), BF16),
        compiler_params=_cparams("parallel", "parallel"),
        name="merge",
    )(x_bf, *outs, wg, bg, wbr)


def _layer_norm(y, g, b):
    mu = jnp.mean(y, axis=-1, keepdims=True)
    var = jnp.mean(jnp.square(y - mu), axis=-1, keepdims=True)
    return (y - mu) * lax.rsqrt(var + LN_EPS) * g + b


def _out_ln_kernel(x_ref, m_ref, w_ref, g_ref, b_ref, of_ref, ob_ref, *, alpha):
    y = alpha * x_ref[...] + jnp.dot(m_ref[...], w_ref[...], preferred_element_type=F32)
    o = _layer_norm(y, g_ref[...], b_ref[...])
    of_ref[...] = o
    ob_ref[...] = o.astype(ob_ref.dtype)


def out_ln(x, merged, w_out, g, b, *, alpha, tm):
    S, D = x.shape
    row = pl.BlockSpec((tm, D), lambda i: (i, 0))
    vec = pl.BlockSpec((1, D), lambda i: (0, 0))
    return pl.pallas_call(
        functools.partial(_out_ln_kernel, alpha=alpha),
        grid=(S // tm,),
        in_specs=[row, row, pl.BlockSpec((D, D), lambda i: (0, 0)), vec, vec],
        out_specs=[row, row],
        out_shape=[jax.ShapeDtypeStruct((S, D), F32), jax.ShapeDtypeStruct((S, D), BF16)],
        compiler_params=_cparams("parallel"),
        name="out_ln",
    )(x, merged, w_out, g, b)


def _add_ln_kernel(x_ref, y_ref, g_ref, b_ref, of_ref, ob_ref, *, alpha):
    o = _layer_norm(alpha * x_ref[...] + y_ref[...], g_ref[...], b_ref[...])
    of_ref[...] = o
    ob_ref[...] = o.astype(ob_ref.dtype)


def add_ln(x, y, g, b, *, alpha, tm):
    S, D = x.shape
    row = pl.BlockSpec((tm, D), lambda i: (i, 0))
    vec = pl.BlockSpec((1, D), lambda i: (0, 0))
    return pl.pallas_call(
        functools.partial(_add_ln_kernel, alpha=alpha),
        grid=(S // tm,),
        in_specs=[row, row, vec, vec],
        out_specs=[row, row],
        out_shape=[jax.ShapeDtypeStruct((S, D), F32), jax.ShapeDtypeStruct((S, D), BF16)],
        compiler_params=_cparams("parallel"),
        name="add_ln",
    )(x, y, g, b)


def _top_sorted(s, k):
    vals = []
    cur = s
    for r in range(k):
        m = jnp.max(cur, axis=0, keepdims=True)
        vals.append(m)
        if r + 1 < k:
            cur = jnp.where(cur == m, -jnp.inf, cur)
    return vals


def _peer_route_kernel(q_ref, keys_ref, s0_ref, s1_ref, e1_ref, c_ref, tau_ref):
    K = PEER_TOPK

    def head(h, _):
        s0 = lax.dot_general(keys_ref[h, 0], q_ref[2 * h], _NT, precision=lax.Precision.HIGHEST,
                             preferred_element_type=F32)
        s1 = lax.dot_general(keys_ref[h, 1], q_ref[2 * h + 1], _NT, precision=lax.Precision.HIGHEST,
                             preferred_element_type=F32)
        v0 = _top_sorted(s0, K)
        v1 = jnp.concatenate(_top_sorted(s1, K), axis=0)
        hk = K // 2
        cands = ([v0[0] + v1[:hk], v0[0] + v1[hk:]]
                 + [v0[a] + v1[:hk] for a in range(1, hk)]
                 + [jnp.concatenate(v0[hk:], axis=0) + v1[0:1]])
        z = None
        for r in range(K):
            m = functools.reduce(jnp.maximum, cands)
            m = jnp.max(m, axis=0, keepdims=True)
            if r == 0:
                top = m
                z = jnp.ones_like(m)
            else:
                z = z + jnp.exp(m - top)
            if r + 1 < K:
                cands = [jnp.where(c == m, -jnp.inf, c) for c in cands]
        c = jnp.exp(s0 - v0[0]) / z
        n_rows = s0_ref.shape[2]
        for n in range(s0_ref.shape[1]):
            s0_ref[h, n] = s0[n * n_rows:(n + 1) * n_rows]
            c_ref[h, n] = c[n * n_rows:(n + 1) * n_rows]
        s1_ref[h] = s1
        e1_ref[h] = jnp.exp(s1 - v1[0:1])
        tau_ref[pl.ds(h, 1), :] = m
        return 0

    lax.fori_loop(0, PEER_HEADS, head, 0)


def peer_route(q_slots, sub_keys, *, S, tq, n_rows):
    arr = jax.ShapeDtypeStruct((PEER_HEADS, PEER_N_KEYS, S), F32)
    spec = pl.BlockSpec((PEER_HEADS, PEER_N_KEYS, tq), lambda i: (0, 0, i))
    n_tiles = PEER_N_KEYS // n_rows
    arr_t = jax.ShapeDtypeStruct((PEER_HEADS, n_tiles, n_rows, S), F32)
    spec_t = pl.BlockSpec((PEER_HEADS, n_tiles, n_rows, tq), lambda i: (0, 0, 0, i))
    return pl.pallas_call(
        _peer_route_kernel,
        grid=(S // tq,),
        in_specs=[pl.BlockSpec((2 * PEER_HEADS, tq, LANE), lambda i: (0, i, 0)),
                  pl.BlockSpec(sub_keys.shape, lambda i: (0, 0, 0, 0))],
        out_specs=[spec_t, spec, spec, spec_t, pl.BlockSpec((PEER_HEADS, tq), lambda i: (0, i))],
        out_shape=[arr_t, arr, arr, arr_t, jax.ShapeDtypeStruct((PEER_HEADS, S), F32)],
        compiler_params=_cparams("parallel"),
        name="peer_route",
    )(q_slots, sub_keys)


def _peer_expert_kernel(x_ref, u_ref, v_ref, s0a_ref, s0b_ref, ca_ref, cb_ref, s1_ref, e1_ref, tau_ref,
                        y_ref, ha_scr, hb_scr, wa_scr, wb_scr):
    k = pl.program_id(1)
    te, tq = ha_scr.shape
    n_rows = te // PEER_N_KEYS
    D = y_ref.shape[1]
    nc = D // n_rows

    @pl.when(k == 0)
    def _():
        y_ref[...] = jnp.zeros_like(y_ref)
        hb_scr[...] = jnp.zeros_like(hb_scr)
        wa_scr[...] = jnp.zeros_like(wa_scr)

    def phase(u_lo, h_out, w_prev, h_prev, w_out, s0_ref, c_ref):
        rb = 32

        def y_chunk(r):
            cols = pl.ds(r * nc, nc)
            y_ref[:, cols] += lax.dot_general(w_prev[...], v_ref[pl.ds(u_lo, te), cols], _TN,
                                              preferred_element_type=F32)

        def h_half(n):
            toks = pl.ds(n * (tq // 2), tq // 2)
            h_out[:, toks] = lax.dot_general(u_ref[pl.ds(u_lo, te), :], x_ref[toks, :], _NT,
                                             preferred_element_type=F32)

        def gate_block(tc, sb):
            tl = pl.ds(tc * LANE, LANE)
            keys = pl.ds(sb * rb, rb)
            gs = [None] * n_rows
            for h in range(PEER_HEADS):
                s1b = s1_ref[h, keys, tl]
                e1b = e1_ref[h, keys, tl]
                taub = tau_ref[pl.ds(h, 1), tl]
                for r in range(n_rows):
                    ssum = s0_ref[h, pl.ds(r, 1), tl] + s1b
                    t = jnp.where(ssum >= taub, e1b, 0.0) * c_ref[h, pl.ds(r, 1), tl]
                    gs[r] = t if gs[r] is None else gs[r] + t
            for r in range(n_rows):
                rows = pl.ds(r * PEER_N_KEYS + sb * rb, rb)
                hp = h_prev[rows, tl]
                act = 0.5 * hp * (1.0 + lax.erf(hp * math.sqrt(0.5)))
                w_out[rows, tl] = (gs[r] * act).astype(w_out.dtype)

        blocks = [(tc, sb) for tc in range(tq // LANE) for sb in range(PEER_N_KEYS // rb)]
        pieces = [functools.partial(y_chunk, r) for r in range(n_rows)] + [functools.partial(h_half, n) for n in range(2)]
        weights = [1] * n_rows + [2, 2]
        per = len(blocks) // sum(weights)
        b0 = 0
        for piece, wgt in zip(pieces, weights):
            piece()
            for tc, sb in blocks[b0:b0 + per * wgt]:
                gate_block(tc, sb)
            b0 += per * wgt
        for tc, sb in blocks[b0:]:
            gate_block(tc, sb)

    phase(0, ha_scr, wa_scr, hb_scr, wb_scr, s0a_ref, ca_ref)
    phase(te, hb_scr, wb_scr, ha_scr, wa_scr, s0b_ref, cb_ref)


def peer_experts(x_bf, u_bf, v_bf, s0_t, s1, e1, c_t, tau, *, tq, te):
    S, D = x_bf.shape
    E = u_bf.shape[0]
    n_tiles = E // te
    n_blocks = n_tiles // 2
    n_rows = te // PEER_N_KEYS
    assert s0_t.shape == c_t.shape == (PEER_HEADS, n_tiles, n_rows, S)
    rt = pl.BlockSpec((PEER_HEADS, PEER_N_KEYS, tq), lambda i, k: (0, 0, i))
    row_a = pl.BlockSpec((PEER_HEADS, None, n_rows, tq), lambda i, k: (0, jnp.maximum(2 * k - 1, 0), 0, i))
    row_b = pl.BlockSpec((PEER_HEADS, None, n_rows, tq),
                         lambda i, k: (0, jnp.minimum(2 * k, n_tiles - 1), 0, i))
    return pl.pallas_call(
        _peer_expert_kernel,
        grid=(S // tq, n_blocks + 1),
        in_specs=[pl.BlockSpec((tq, D), lambda i, k: (i, 0)),
                  pl.BlockSpec((2 * te, D), lambda i, k: (jnp.minimum(k, n_blocks - 1), 0)),
                  pl.BlockSpec((2 * te, D), lambda i, k: (jnp.maximum(k - 1, 0), 0)),
                  row_a, row_b, row_a, row_b, rt, rt,
                  pl.BlockSpec((PEER_HEADS, tq), lambda i, k: (0, i))],
        out_specs=pl.BlockSpec((tq, D), lambda i, k: (i, 0)),
        out_shape=jax.ShapeDtypeStruct((S, D), F32),
        scratch_shapes=[pltpu.VMEM((te, tq), F32), pltpu.VMEM((te, tq), F32),
                        pltpu.VMEM((te, tq), BF16), pltpu.VMEM((te, tq), BF16)],
        compiler_params=_cparams("parallel", "arbitrary"),
        name="peer_experts",
    )(x_bf, u_bf, v_bf, s0_t, s0_t, c_t, c_t, s1, e1, tau)


def _pad_cols(w, width):
    return jnp.pad(w, ((0, 0), (0, width - w.shape[1])))


def _rot_cols(w):
    h = w.shape[1] // 2
    return jnp.concatenate([-w[:, h:], w[:, :h]], axis=1)


def _split_w_in(w):
    widths = (512, 512, 512, 512, 128, 128, 512, 512, 512, D_Q_LORA, D_KV_LORA, D_ROPE)
    out, o = [], 0
    for n in widths:
        out.append(w[:, o:o + n])
        o += n
    return out


def _layer_weights(w_in, a_scale, b_scale, c_scale):
    (a_q, a_k, a_v, b_q, b_k, b_v, c_q, c_k, c_v, d_cq, d_ckv, d_kr) = _split_w_in(w_in)
    b_k0, b_k1 = b_k[:, :64], b_k[:, 64:]
    b_v0, b_v1 = b_v[:, :64], b_v[:, 64:]
    w_abc = jnp.concatenate(
        [a_q, a_k, a_v,
         b_q, _pad_cols(b_k0, LANE), _pad_cols(b_k1, LANE), _pad_cols(b_v0, LANE), _pad_cols(b_v1, LANE),
         c_q, c_k, c_v], axis=1).astype(BF16)
    ones = lambda n: jnp.ones((n,), F32)
    s_abc = jnp.concatenate([ones(512) * a_scale, ones(1024),
                             ones(512) * b_scale, ones(512),
                             ones(512) * c_scale, ones(1024)])[None, :]
    w_d = jnp.concatenate([d_cq, d_ckv, _pad_cols(d_kr, LANE), _pad_cols(_rot_cols(d_kr), LANE)],
                          axis=1).astype(BF16)
    return w_abc, s_abc, w_d


def _mla_weights(w_uq, w_ukv):
    wq = w_uq.reshape(D_Q_LORA, D_HEADS, D_NOPE + D_ROPE)
    nope, pe = wq[..., :D_NOPE], wq[..., D_NOPE:]
    pe_rot = jnp.concatenate([-pe[..., D_ROPE // 2:], pe[..., :D_ROPE // 2]], axis=-1)
    z64 = jnp.zeros((D_Q_LORA, D_HEADS, LANE - D_ROPE), F32)
    z128 = jnp.zeros((D_Q_LORA, D_HEADS, D_NOPE), F32)
    wqa = jnp.concatenate([nope, pe, z64], axis=-1).reshape(D_Q_LORA, -1).astype(BF16)
    wqb = jnp.concatenate([z128, pe_rot, z64], axis=-1).reshape(D_Q_LORA, -1).astype(BF16)
    wkv = w_ukv.reshape(D_KV_LORA, D_HEADS, D_NOPE + D_V)
    wkv = jnp.concatenate([wkv[..., :D_NOPE].reshape(D_KV_LORA, -1),
                           wkv[..., D_NOPE:].reshape(D_KV_LORA, -1)], axis=1).astype(BF16)
    return wqa, wqb, wkv


def _rope_tables(positions):
    half = D_ROPE // 2
    inv = ROPE_THETA ** (-jnp.arange(half, dtype=F32) / half)
    ang = positions.astype(F32)[..., None] * inv
    cos, sin = jnp.cos(ang), jnp.sin(ang)
    S = positions.shape[0]
    one = jnp.ones((S, D_NOPE), F32)
    z = jnp.zeros((S, LANE - D_ROPE), F32)
    cosm = jnp.concatenate([one, cos, cos, z], axis=1)
    sinm = jnp.concatenate([jnp.zeros((S, D_NOPE), F32), sin, sin, z], axis=1)
    return cosm, sinm


def kernel(x, positions, t5_table, w_in, w_gate, b_gate, w_branch, w_out, swa_sinks, diff_lambda_q1, diff_lambda_k1, diff_lambda_q2, diff_lambda_k2, diff_subln_g, mla_q_norm_g, mla_kv_norm_g, mla_w_uq, mla_w_ukv, ln1_g, ln1_b, peer_w_q, peer_sub_keys, peer_u, peer_v, ln2_g, ln2_b):
    B, S, D = x.shape
    assert B == 1 and D == D_MODEL and S % MOBA_BLOCK == 0
    depth = w_in.shape[0]
    alpha = (2.0 * depth) ** 0.25
    tm = min(512, S)
    T = min(512, S)

    tab_a = t5_table[:, :A_HEADS]
    tab_b = t5_table[:, A_HEADS:A_HEADS + B_HEADS]
    tab_c = t5_table[:, A_HEADS + B_HEADS:]
    bias_a = t5_tiles(tab_a, T=T, n_near=_n_near(T))[:, None]
    bias_c = t5_tiles(tab_c, T=T, n_near=_n_near(T))[:, None]
    band_b = t5_band(tab_b, tq=tm, window=SWA_WINDOW)
    cosm, sinm = _rope_tables(positions[0])

    xf = x[0]
    xb = xf.astype(BF16)
    for layer in range(depth):
        lam_init = 0.8 - 0.6 * math.exp(-0.3 * layer)
        w_abc, s_abc, w_d = _layer_weights(w_in[layer], 128 ** -0.5 * LOG2E, B_HEAD_DIM ** -0.5 * LOG2E,
                                           C_HEAD_DIM ** -0.5 * LOG2E)
        hs = proj_slots(xb, w_abc, s_abc, BF16, tm=tm, slots_per_step=8)
        hd = proj_slots(xb, w_d, jnp.ones((1, w_d.shape[1]), F32), F32, tm=tm, slots_per_step=8)

        nb = S // MOBA_BLOCK
        nbp = -(-nb // LANE) * LANE
        kmean = jnp.pad(moba_kmean(hs, 4, S), ((0, 0), (0, nbp - nb), (0, 0)))
        selb = moba_select(hs, 0, kmean, S, tq=min(1024, S))
        o_a = flash_pipe("A", hs, hs, hs, S=S, T=T, q_slot=lambda u: u, k_slot=lambda u: 4 + u,
                    v_slot=lambda u: 8 + u, bias=bias_a, selb=selb)
        sinks = ((swa_sinks[layer] - tab_b[T5_BUCKETS - 1]) * LOG2E).reshape(B_HEADS // 2, 2)
        o_b = swa_attention(hs, band_b, sinks, S=S, tq=tm, q_slot0=12, k_slot0=16, v_slot0=18)
        lam_params = jnp.stack([diff_lambda_q1[layer], diff_lambda_k1[layer],
                                diff_lambda_q2[layer], diff_lambda_k2[layer]])
        o_c = flash_pipe("C", hs, hs, hs, S=S, T=T, q_slot=lambda u: 20 + u, k_slot=lambda u: 24 + u,
                    v_slot=lambda u: 28 + u, bias=bias_c, lam_params=lam_params,
                    subln_g=diff_subln_g[layer][None, :], lam_init=lam_init)
        wqa, wqb, wkv = _mla_weights(mla_w_uq[layer], mla_w_ukv[layer])
        q_d, k_d, v_d = mla_prep(hd, mla_q_norm_g[layer][None, :], mla_kv_norm_g[layer][None, :],
                                 wqa, wqb, wkv, cosm, sinm, S=S, tm=tm)
        o_d = flash_pipe("D", q_d, k_d, v_d, S=S, T=T, q_slot=lambda u: u, k_slot=lambda u: u,
                    v_slot=lambda u: u)

        merged = merge(xb, (o_a, o_b, o_c, o_d),
                       w_gate[layer].astype(BF16), b_gate[layer][:, None, :], w_branch[layer].astype(BF16), tm=tm, tn=512)
        xf, xb = out_ln(xf, merged, w_out[layer].astype(BF16), ln1_g[layer][None, :],
                        ln1_b[layer][None, :], alpha=alpha, tm=tm)

        q_slots = proj_slots(xb, peer_w_q[layer].astype(BF16), jnp.ones((1, D), F32), F32,
                             tm=tm, slots_per_step=8)
        te = 512
        s0_t, s1, e1, c_t, tau = peer_route(q_slots, peer_sub_keys[layer], S=S, tq=tm,
                                            n_rows=te // PEER_N_KEYS)
        y = peer_experts(xb, peer_u[layer].astype(BF16), peer_v[layer].astype(BF16),
                         s0_t, s1, e1, c_t, tau, tq=tm, te=te)
        xf, xb = add_ln(xf, y, ln2_g[layer][None, :], ln2_b[layer][None, :], alpha=alpha, tm=tm)
    return xf[None]
```

```python
import functools
import math

import jax
import jax.numpy as jnp
from jax import lax
from jax.experimental import pallas as pl
from jax.experimental.pallas import tpu as pltpu

F32 = jnp.float32
BF16 = jnp.bfloat16
LANE = 128
NEG = -1e30
LOG2E = math.log2(math.e)

D_MODEL = 2048
A_HEADS = 4
MOBA_BLOCK = 256
MOBA_TOPK = 3
B_HEADS = 8
B_KV_HEADS = 2
B_HEAD_DIM = 64
SWA_WINDOW = 128
C_HEADS = 4
C_HEAD_DIM = 64
D_HEADS = 4
D_Q_LORA = 512
D_KV_LORA = 256
D_NOPE = 128
D_ROPE = 64
D_V = 128
ROPE_THETA = 10000.0
N_BRANCHES = 4
BRANCH_W = 512
T5_BUCKETS = 32
T5_MAX_EXACT = T5_BUCKETS // 2
T5_MAX_DISTANCE = 1024
PEER_HEADS = 8
PEER_N_KEYS = 128
PEER_D_KEY = 128
PEER_TOPK = 16
LN_EPS = 1e-5
RMS_EPS = 1e-6

VMEM_LIMIT_MB = 56


def _cparams(*sem):
    return pltpu.CompilerParams(dimension_semantics=sem,
                                vmem_limit_bytes=VMEM_LIMIT_MB * 1024 * 1024)


_NT = (((1,), (1,)), ((), ()))
_TN = (((0,), (0,)), ((), ()))


def _proj_kernel(x_ref, w_ref, s_ref, o_ref):
    acc = jnp.dot(x_ref[...], w_ref[...], preferred_element_type=F32) * s_ref[...]
    for c in range(o_ref.shape[0]):
        o_ref[c] = acc[:, c * LANE:(c + 1) * LANE].astype(o_ref.dtype)


def proj_slots(x_bf, w_bf, colscale, out_dtype, *, tm, slots_per_step):
    S, K = x_bf.shape
    N = w_bf.shape[1]
    tn = slots_per_step * LANE
    return pl.pallas_call(
        _proj_kernel,
        grid=(N // tn, S // tm),
        in_specs=[pl.BlockSpec((tm, K), lambda j, i: (i, 0)),
                  pl.BlockSpec((K, tn), lambda j, i: (0, j)),
                  pl.BlockSpec((1, tn), lambda j, i: (0, j))],
        out_specs=pl.BlockSpec((slots_per_step, tm, LANE), lambda j, i: (j, i, 0)),
        out_shape=jax.ShapeDtypeStruct((N // LANE, S, LANE), out_dtype),
        compiler_params=_cparams("parallel", "parallel"),
        name="proj_slots",
    )(x_bf, w_bf, colscale)


def _t5_tiles_kernel(tab_ref, o_ref, *, T, n_heads):
    dj = pl.program_id(0)
    row = lax.broadcasted_iota(jnp.int32, (T, T), 0)
    col = lax.broadcasted_iota(jnp.int32, (T, T), 1)
    bucket = _t5_bucket(jnp.maximum(row - col + dj * T, 0))
    for h in range(n_heads):
        last = tab_ref[T5_BUCKETS - 1, h]
        acc = jnp.zeros((T, T), F32)
        for b in range(T5_BUCKETS - 1):
            acc = jnp.where(bucket == b, (tab_ref[b, h] - last) * LOG2E, acc)
        o_ref[h, 0] = acc


def t5_tiles(tab, *, T, n_near):
    H = tab.shape[1]
    return pl.pallas_call(
        functools.partial(_t5_tiles_kernel, T=T, n_heads=H),
        grid=(n_near,),
        in_specs=[pl.BlockSpec(memory_space=pltpu.SMEM)],
        out_specs=pl.BlockSpec((H, 1, T, T), lambda d: (0, d, 0, 0)),
        out_shape=jax.ShapeDtypeStruct((H, n_near, T, T), F32),
        compiler_params=_cparams("parallel"),
        name="t5_tiles",
    )(tab)


def _n_near(T):
    return (T5_MAX_DISTANCE - 1 + T - 1) // T + 1


def _t5_bucket(n):
    nf = jnp.maximum(n, 1).astype(F32)
    large = T5_MAX_EXACT + (jnp.log(nf / T5_MAX_EXACT) / math.log(T5_MAX_DISTANCE / T5_MAX_EXACT)
                            * (T5_BUCKETS - T5_MAX_EXACT)).astype(jnp.int32)
    return jnp.where(n < T5_MAX_EXACT, n, jnp.minimum(large, T5_BUCKETS - 1))


def _t5_band_kernel(tab_ref, o_ref, *, window):
    h = pl.program_id(0)
    rows, cols = o_ref.shape
    dist = (lax.broadcasted_iota(jnp.int32, (rows, cols), 0)
            - lax.broadcasted_iota(jnp.int32, (rows, cols), 1) + window)
    bucket = _t5_bucket(jnp.maximum(dist, 0))
    last = tab_ref[T5_BUCKETS - 1, h]
    acc = jnp.zeros((rows, cols), F32)
    for b in range(T5_BUCKETS - 1):
        acc = jnp.where(bucket == b, (tab_ref[b, h] - last) * LOG2E, acc)
    o_ref[...] = jnp.where((dist >= 0) & (dist < window), acc, NEG)


def t5_band(tab, *, tq, window):
    H = tab.shape[1]
    cols = tq + 2 * window
    return pl.pallas_call(
        functools.partial(_t5_band_kernel, window=window),
        grid=(H,),
        in_specs=[pl.BlockSpec(memory_space=pltpu.SMEM)],
        out_specs=pl.BlockSpec((None, tq, cols), lambda h: (h, 0, 0)),
        out_shape=jax.ShapeDtypeStruct((H, tq, cols), F32),
        compiler_params=_cparams("parallel"),
        name="t5_band",
    )(tab)


def _swa_kernel(q_ref, k_ref, v_ref, band_ref, sink_ref, o_ref, *, window):
    u = pl.program_id(0)
    qi = pl.program_id(1)
    tq = q_ref.shape[0]
    nk = tq + window
    half = LANE // 2
    start = pl.multiple_of(jnp.maximum(qi * tq - window, 0), window)
    kb = k_ref[pl.ds(start, nk), :]
    vb = v_ref[pl.ds(start, nk), :]
    q = q_ref[...]
    z = jnp.zeros((tq, half), q.dtype)
    outs = []
    for c in range(2):
        qc = jnp.concatenate([q[:, c * half:(c + 1) * half], z], axis=1)
        band = jnp.where(qi == 0, band_ref[c, :, window:], band_ref[c, :, :nk])
        s = lax.dot_general(qc, kb, _NT, preferred_element_type=F32) + band
        sink = sink_ref[u, c]
        m = jnp.maximum(jnp.max(s, axis=-1, keepdims=True), sink)
        p = jnp.exp2(s - m)
        l = jnp.sum(p, axis=-1, keepdims=True) + jnp.exp2(sink - m)
        o = jnp.dot(p.astype(vb.dtype), vb, preferred_element_type=F32) / l
        outs.append(o[:, :half])
    o_ref[...] = jnp.concatenate(outs, axis=1).astype(o_ref.dtype)


def swa_attention(hs, band, sinks, *, S, tq, q_slot0, k_slot0, v_slot0):
    n_units = B_HEADS // 2
    per_kv = n_units // B_KV_HEADS
    cols = band.shape[-1]
    return pl.pallas_call(
        functools.partial(_swa_kernel, window=SWA_WINDOW),
        grid=(n_units, S // tq),
        in_specs=[pl.BlockSpec((None, tq, LANE), lambda u, i: (q_slot0 + u, i, 0)),
                  pl.BlockSpec((None, S, LANE), lambda u, i: (k_slot0 + u // per_kv, 0, 0)),
                  pl.BlockSpec((None, S, LANE), lambda u, i: (v_slot0 + u // per_kv, 0, 0)),
                  pl.BlockSpec((2, tq, cols), lambda u, i: (u, 0, 0)),
                  pl.BlockSpec(memory_space=pltpu.SMEM)],
        out_specs=pl.BlockSpec((None, tq, LANE), lambda u, i: (u, i, 0)),
        out_shape=jax.ShapeDtypeStruct((n_units, S, LANE), BF16),
        compiler_params=_cparams("parallel", "arbitrary"),
        name="swa",
    )(hs, hs, hs, band, sinks)


def _kmean_kernel(k_ref, o_ref, *, blk):
    k = k_ref[...].astype(F32)
    nb = k.shape[0] // blk
    o_ref[...] = k.reshape(nb, blk, LANE).sum(axis=1) * (1.0 / blk)


def moba_kmean(hs, k_slot0, S):
    nb = S // MOBA_BLOCK
    return pl.pallas_call(
        functools.partial(_kmean_kernel, blk=MOBA_BLOCK),
        grid=(A_HEADS,),
        in_specs=[pl.BlockSpec((None, S, LANE), lambda h: (k_slot0 + h, 0, 0))],
        out_specs=pl.BlockSpec((None, nb, LANE), lambda h: (h, 0, 0)),
        out_shape=jax.ShapeDtypeStruct((A_HEADS, nb, LANE), F32),
        compiler_params=_cparams("parallel"),
        name="moba_kmean",
    )(hs)


def _moba_select_kernel(q_ref, km_ref, o_ref, *, topk):
    tq = q_ref.shape[0]
    q = q_ref[...].astype(F32)
    gate = lax.dot_general(q, km_ref[...], _NT, precision=lax.Precision.HIGHEST,
                           preferred_element_type=F32)
    lane = lax.broadcasted_iota(jnp.int32, gate.shape, 1)
    row = lax.broadcasted_iota(jnp.int32, gate.shape, 0)
    own = pl.program_id(1) * (tq // MOBA_BLOCK) + row // MOBA_BLOCK
    nbp = gate.shape[1]
    gate = jnp.where(lane < own, gate, -jnp.inf)
    mask = jnp.where(lane == own, 0.0, NEG)
    for _ in range(topk):
        mx = jnp.max(gate, axis=-1, keepdims=True)
        idx = jnp.min(jnp.where(gate == mx, lane, nbp), axis=-1, keepdims=True)
        hit = lane == idx
        mask = jnp.where(hit, jnp.where(mx > -jnp.inf, 0.0, mask), mask)
        gate = jnp.where(hit, -jnp.inf, gate)
    o_ref[...] = mask.astype(o_ref.dtype)


def moba_select(hs, q_slot0, kmean_pad, S, *, tq):
    nbp = kmean_pad.shape[1]
    return pl.pallas_call(
        functools.partial(_moba_select_kernel, topk=MOBA_TOPK),
        grid=(A_HEADS, S // tq),
        in_specs=[pl.BlockSpec((None, tq, LANE), lambda h, i: (q_slot0 + h, i, 0)),
                  pl.BlockSpec((None, nbp, LANE), lambda h, i: (h, 0, 0))],
        out_specs=pl.BlockSpec((None, tq, nbp), lambda h, i: (h, i, 0)),
        out_shape=jax.ShapeDtypeStruct((A_HEADS, S, nbp), BF16),
        compiler_params=_cparams("parallel", "parallel"),
        name="moba_select",
    )(hs, kmean_pad)


def _flash_pipe_kernel(*refs, mode, T, n_near, lam_init):
    it = iter(refs)
    q_ref, k_ref, v_ref = next(it), next(it), next(it)
    bias_ref = next(it) if mode in "AC" else None
    selb_ref = next(it) if mode == "A" else None
    lam_ref, g_ref = (next(it), next(it)) if mode == "C" else (None, None)
    o_ref = next(it)
    m_scr, l_scr, acc_scr = next(it), next(it), next(it)
    rest = list(it)
    s_scr, p_scr = rest[:len(rest) // 2], rest[len(rest) // 2:]

    qi = pl.program_id(1)
    q = q_ref[...]
    half = LANE // 2
    if mode == "A":
        qs = [jnp.concatenate([q, selb_ref[...]], axis=1)]
    elif mode == "C":
        lane = lax.broadcasted_iota(jnp.int32, q.shape, 1)
        zq = jnp.zeros_like(q)
        qs = [jnp.where(lane < half, q, zq), jnp.where(lane >= half, q, zq)]
    else:
        qs = [q]
    n_sub = len(qs)

    def k_tile(j):
        kt = k_ref[pl.ds(pl.multiple_of(j * T, T), T), :]
        if mode == "A":
            n_blk = selb_ref.shape[-1]
            blk = lax.broadcasted_iota(jnp.int32, (T, n_blk), 1)
            row = lax.broadcasted_iota(jnp.int32, (T, n_blk), 0)
            own = j * (T // MOBA_BLOCK) + row // MOBA_BLOCK
            kt = jnp.concatenate([kt, jnp.where(blk == own, 1.0, 0.0).astype(kt.dtype)], axis=1)
        return kt

    def v_tile(j):
        return v_ref[pl.ds(pl.multiple_of(j * T, T), T), :]

    def scores(c, kt):
        return lax.dot_general(qs[c], kt, _NT, preferred_element_type=F32)

    kt0 = k_tile(0)
    for c in range(n_sub):
        s_scr[c][0] = scores(c, kt0)
        p_scr[c][1] = jnp.zeros((T, T), p_scr[c].dtype)
        m_scr[c] = jnp.full((T, LANE), NEG, F32)
        l_scr[c] = jnp.zeros((T, LANE), F32)
        acc_scr[c] = jnp.zeros(acc_scr.shape[1:], F32)

    def step(j, slot, near):
        nslot = 1 - slot
        kt_next = k_tile(jnp.minimum(j + 1, qi))
        vt_prev = v_tile(jnp.maximum(j - 1, 0))
        if near:
            dj = qi - j
            row = lax.broadcasted_iota(jnp.int32, (T, T), 0)
            col = lax.broadcasted_iota(jnp.int32, (T, T), 1)
            valid = row - col + dj * T >= 0
        for c in range(n_sub):
            pv = jnp.dot(p_scr[c][nslot], vt_prev, preferred_element_type=F32)
            s = s_scr[c][slot]
            s_scr[c][nslot] = scores(c, kt_next)
            if near:
                if bias_ref is not None:
                    s = s + bias_ref[0, dj]
                s = jnp.where(valid, s, NEG)
            m_prev = m_scr[c]
            m_new = jnp.maximum(m_prev, jnp.max(s, axis=-1, keepdims=True))
            alpha = jnp.exp2(m_prev - m_new)
            ps = [jnp.exp2(s[:, i * LANE:(i + 1) * LANE] - m_new) for i in range(T // LANE)]
            p_scr[c][slot] = jnp.concatenate(ps, axis=1).astype(p_scr[c].dtype)
            l_scr[c] = alpha * l_scr[c] + functools.reduce(lambda a, b: a + b, ps)
            acc_scr[c] = alpha * (acc_scr[c] + pv)
            m_scr[c] = m_new

    n_pairs = jnp.maximum(qi - n_near + 1, 0) // 2

    def far_pair(i, _):
        step(2 * i, 0, False)
        step(2 * i + 1, 1, False)
        return 0

    lax.fori_loop(0, n_pairs, far_pair, 0)
    for t in range(n_near + 1):
        jn = 2 * n_pairs + t

        @pl.when(jn <= qi)
        def _():
            step(jn, t % 2, True)

    last = lax.rem(qi, 2)
    vt = v_tile(qi)
    outs = []
    for c in range(n_sub):
        acc = acc_scr[c] + jnp.dot(p_scr[c][last], vt, preferred_element_type=F32)
        outs.append(acc / jnp.sum(l_scr[c], axis=-1, keepdims=True))
    if mode == "C":
        lp = lam_ref[...]
        lam = (jnp.exp(jnp.sum(lp[0:1] * lp[1:2], axis=-1, keepdims=True))
               - jnp.exp(jnp.sum(lp[2:3] * lp[3:4], axis=-1, keepdims=True)) + lam_init)
        o = outs[0] - lam * outs[1]
        o = o * lax.rsqrt(jnp.mean(o * o, axis=-1, keepdims=True) + RMS_EPS) * g_ref[...]
        o = o * (1.0 - lam_init)
    else:
        o = outs[0]
    o_ref[...] = o.astype(o_ref.dtype)


def flash_pipe(mode, q_arr, k_arr, v_arr, *, S, T, q_slot, k_slot, v_slot, n_units=4, bias=None,
               selb=None, lam_params=None, subln_g=None, lam_init=0.0):
    dq = q_arr.shape[-1]
    dk = k_arr.shape[-1]
    dv = v_arr.shape[-1]
    n_near = 1 if mode == "D" else _n_near(T)
    n_sub = 2 if mode == "C" else 1
    in_specs = [pl.BlockSpec((None, T, dq), lambda u, i: (q_slot(u), i, 0)),
                pl.BlockSpec((None, S, dk), lambda u, i: (k_slot(u), 0, 0)),
                pl.BlockSpec((None, S, dv), lambda u, i: (v_slot(u), 0, 0))]
    args = [q_arr, k_arr, v_arr]
    if bias is not None:
        in_specs.append(pl.BlockSpec((None,) + bias.shape[1:], lambda u, i: (u, 0, 0, 0, 0)))
        args.append(bias)
    if selb is not None:
        in_specs.append(pl.BlockSpec((None, T, selb.shape[-1]), lambda u, i: (u, i, 0)))
        args.append(selb)
    if lam_params is not None:
        in_specs.append(pl.BlockSpec(lam_params.shape, lambda u, i: (0, 0)))
        in_specs.append(pl.BlockSpec(subln_g.shape, lambda u, i: (0, 0)))
        args += [lam_params, subln_g]
    return pl.pallas_call(
        functools.partial(_flash_pipe_kernel, mode=mode, T=T, n_near=n_near, lam_init=lam_init),
        grid=(n_units, S // T),
        in_specs=in_specs,
        out_specs=pl.BlockSpec((None, T, LANE), lambda u, i: (u, i, 0)),
        out_shape=jax.ShapeDtypeStruct((n_units, S, LANE), BF16),
        scratch_shapes=([pltpu.VMEM((n_sub, T, LANE), F32), pltpu.VMEM((n_sub, T, LANE), F32),
                         pltpu.VMEM((n_sub, T, dv), F32)]
                        + [pltpu.VMEM((2, T, T), F32)] * n_sub + [pltpu.VMEM((2, T, T), BF16)] * n_sub),
        compiler_params=_cparams("parallel", "arbitrary"),
        name="flashp_" + mode,
    )(*args)


def _rms(x, g):
    return x * lax.rsqrt(jnp.mean(x * x, axis=-1, keepdims=True) + RMS_EPS) * g


def _mla_prep_kernel(hd_ref, gq_ref, gkv_ref, wqa_ref, wqb_ref, wkv_ref, cos_ref, sin_ref,
                     qo_ref, ko_ref, vo_ref, *, scale):
    cosm = cos_ref[...]
    sinm = sin_ref[...]
    cq = jnp.concatenate([hd_ref[c] for c in range(4)], axis=1)
    cqn = _rms(cq, gq_ref[...]).astype(BF16)
    qa = jnp.dot(cqn, wqa_ref[...], preferred_element_type=F32)
    qb = jnp.dot(cqn, wqb_ref[...], preferred_element_type=F32)
    w = 2 * LANE
    for h in range(D_HEADS):
        qe = (qa[:, h * w:(h + 1) * w] * cosm + qb[:, h * w:(h + 1) * w] * sinm) * scale
        qo_ref[h] = qe.astype(qo_ref.dtype)
    ckv = jnp.concatenate([hd_ref[4], hd_ref[5]], axis=1)
    ckvn = _rms(ckv, gkv_ref[...]).astype(BF16)
    kv = jnp.dot(ckvn, wkv_ref[...], preferred_element_type=F32)
    kpe = hd_ref[6] * cosm[:, LANE:] + hd_ref[7] * sinm[:, LANE:]
    for h in range(D_HEADS):
        ko_ref[h] = jnp.concatenate([kv[:, h * LANE:(h + 1) * LANE], kpe], axis=1).astype(ko_ref.dtype)
        vo_ref[h] = kv[:, (D_HEADS + h) * LANE:(D_HEADS + h + 1) * LANE].astype(vo_ref.dtype)


def mla_prep(hd, gq, gkv, wqa, wqb, wkv, cosm, sinm, *, S, tm):
    w = 2 * LANE
    full = lambda a: pl.BlockSpec(a.shape, lambda i: (0,) * a.ndim)
    return pl.pallas_call(
        functools.partial(_mla_prep_kernel, scale=(D_NOPE + D_ROPE) ** -0.5 * LOG2E),
        grid=(S // tm,),
        in_specs=[pl.BlockSpec((8, tm, LANE), lambda i: (0, i, 0)),
                  full(gq), full(gkv), full(wqa), full(wqb), full(wkv),
                  pl.BlockSpec((tm, w), lambda i: (i, 0)),
                  pl.BlockSpec((tm, w), lambda i: (i, 0))],
        out_specs=[pl.BlockSpec((D_HEADS, tm, w), lambda i: (0, i, 0)),
                   pl.BlockSpec((D_HEADS, tm, w), lambda i: (0, i, 0)),
                   pl.BlockSpec((D_HEADS, tm, LANE), lambda i: (0, i, 0))],
        out_shape=[jax.ShapeDtypeStruct((D_HEADS, S, w), BF16),
                   jax.ShapeDtypeStruct((D_HEADS, S, w), BF16),
                   jax.ShapeDtypeStruct((D_HEADS, S, LANE), BF16)],
        compiler_params=_cparams("parallel"),
        name="mla_prep",
    )(hd, gq, gkv, wqa, wqb, wkv, cosm, sinm)


def _merge_kernel(x_ref, oa_ref, ob_ref, oc_ref, od_ref, wg_ref, bg_ref, wbr_ref, o_ref):
    x = x_ref[...]
    acc = None
    for n, o in enumerate((oa_ref, ob_ref, oc_ref, od_ref)):
        g = jnp.dot(x, wg_ref[n], preferred_element_type=F32) + bg_ref[n]
        ocat = jnp.concatenate([o[c] for c in range(4)], axis=1)
        b = jnp.dot(ocat, wbr_ref[n], preferred_element_type=F32)
        t = jax.nn.sigmoid(g) * b
        acc = t if acc is None else acc + t
    o_ref[...] = acc.astype(o_ref.dtype)


def merge(x_bf, outs, wg, bg, wbr, *, tm, tn):
    S, D = x_bf.shape
    o_spec = pl.BlockSpec((4, tm, LANE), lambda j, i: (0, i, 0))
    return pl.pallas_call(
        _merge_kernel,
        grid=(D // tn, S // tm),
        in_specs=[pl.BlockSpec((tm, D), lambda j, i: (i, 0)), o_spec, o_spec, o_spec, o_spec,
                  pl.BlockSpec((N_BRANCHES, D, tn), lambda j, i: (0, 0, j)),
                  pl.BlockSpec((N_BRANCHES, 1, tn), lambda j, i: (0, 0, j)),
                  pl.BlockSpec((N_BRANCHES, BRANCH_W, tn), lambda j, i: (0, 0, j))],
        out_specs=pl.BlockSpec((tm, tn), lambda j, i: (i, j)),
        out_shape=jax.ShapeDtypeStruct((S, D), BF16),
        compiler_params=_cparams("parallel", "parallel"),
        name="merge",
    )(x_bf, *outs, wg, bg, wbr)


def _layer_norm(y, g, b):
    mu = jnp.mean(y, axis=-1, keepdims=True)
    var = jnp.mean(jnp.square(y - mu), axis=-1, keepdims=True)
    return (y - mu) * lax.rsqrt(var + LN_EPS) * g + b


def _out_ln_kernel(x_ref, m_ref, w_ref, g_ref, b_ref, of_ref, ob_ref, *, alpha):
    y = alpha * x_ref[...] + jnp.dot(m_ref[...], w_ref[...], preferred_element_type=F32)
    o = _layer_norm(y, g_ref[...], b_ref[...])
    of_ref[...] = o
    ob_ref[...] = o.astype(ob_ref.dtype)


def out_ln(x, merged, w_out, g, b, *, alpha, tm):
    S, D = x.shape
    row = pl.BlockSpec((tm, D), lambda i: (i, 0))
    vec = pl.BlockSpec((1, D), lambda i: (0, 0))
    return pl.pallas_call(
        functools.partial(_out_ln_kernel, alpha=alpha),
        grid=(S // tm,),
        in_specs=[row, row, pl.BlockSpec((D, D), lambda i: (0, 0)), vec, vec],
        out_specs=[row, row],
        out_shape=[jax.ShapeDtypeStruct((S, D), F32), jax.ShapeDtypeStruct((S, D), BF16)],
        compiler_params=_cparams("parallel"),
        name="out_ln",
    )(x, merged, w_out, g, b)


def _add_ln_kernel(x_ref, y_ref, g_ref, b_ref, of_ref, ob_ref, *, alpha):
    o = _layer_norm(alpha * x_ref[...] + y_ref[...], g_ref[...], b_ref[...])
    of_ref[...] = o
    ob_ref[...] = o.astype(ob_ref.dtype)


def add_ln(x, y, g, b, *, alpha, tm):
    S, D = x.shape
    row = pl.BlockSpec((tm, D), lambda i: (i, 0))
    vec = pl.BlockSpec((1, D), lambda i: (0, 0))
    return pl.pallas_call(
        functools.partial(_add_ln_kernel, alpha=alpha),
        grid=(S // tm,),
        in_specs=[row, row, vec, vec],
        out_specs=[row, row],
        out_shape=[jax.ShapeDtypeStruct((S, D), F32), jax.ShapeDtypeStruct((S, D), BF16)],
        compiler_params=_cparams("parallel"),
        name="add_ln",
    )(x, y, g, b)


def _top_sorted(s, k):
    vals = []
    cur = s
    for r in range(k):
        m = jnp.max(cur, axis=0, keepdims=True)
        vals.append(m)
        if r + 1 < k:
            cur = jnp.where(cur == m, -jnp.inf, cur)
    return vals


def _peer_route_kernel(q_ref, keys_ref, s0_ref, s1_ref, e1_ref, c_ref, tau_ref):
    K = PEER_TOPK

    def head(h, _):
        s0 = lax.dot_general(keys_ref[h, 0], q_ref[2 * h], _NT, precision=lax.Precision.HIGHEST,
                             preferred_element_type=F32)
        s1 = lax.dot_general(keys_ref[h, 1], q_ref[2 * h + 1], _NT, precision=lax.Precision.HIGHEST,
                             preferred_element_type=F32)
        v0 = _top_sorted(s0, K)
        v1 = jnp.concatenate(_top_sorted(s1, K), axis=0)
        hk = K // 2
        cands = ([v0[0] + v1[:hk], v0[0] + v1[hk:]]
                 + [v0[a] + v1[:hk] for a in range(1, hk)]
                 + [jnp.concatenate(v0[hk:], axis=0) + v1[0:1]])
        z = None
        for r in range(K):
            m = functools.reduce(jnp.maximum, cands)
            m = jnp.max(m, axis=0, keepdims=True)
            if r == 0:
                top = m
                z = jnp.ones_like(m)
            else:
                z = z + jnp.exp(m - top)
            if r + 1 < K:
                cands = [jnp.where(c == m, -jnp.inf, c) for c in cands]
        c = jnp.exp(s0 - v0[0]) / z
        n_rows = s0_ref.shape[2]
        for n in range(s0_ref.shape[1]):
            s0_ref[h, n] = s0[n * n_rows:(n + 1) * n_rows]
            c_ref[h, n] = c[n * n_rows:(n + 1) * n_rows]
        s1_ref[h] = s1
        e1_ref[h] = jnp.exp(s1 - v1[0:1])
        tau_ref[pl.ds(h, 1), :] = m
        return 0

    lax.fori_loop(0, PEER_HEADS, head, 0)


def peer_route(q_slots, sub_keys, *, S, tq, n_rows):
    arr = jax.ShapeDtypeStruct((PEER_HEADS, PEER_N_KEYS, S), F32)
    spec = pl.BlockSpec((PEER_HEADS, PEER_N_KEYS, tq), lambda i: (0, 0, i))
    n_tiles = PEER_N_KEYS // n_rows
    arr_t = jax.ShapeDtypeStruct((PEER_HEADS, n_tiles, n_rows, S), F32)
    spec_t = pl.BlockSpec((PEER_HEADS, n_tiles, n_rows, tq), lambda i: (0, 0, 0, i))
    return pl.pallas_call(
        _peer_route_kernel,
        grid=(S // tq,),
        in_specs=[pl.BlockSpec((2 * PEER_HEADS, tq, LANE), lambda i: (0, i, 0)),
                  pl.BlockSpec(sub_keys.shape, lambda i: (0, 0, 0, 0))],
        out_specs=[spec_t, spec, spec, spec_t, pl.BlockSpec((PEER_HEADS, tq), lambda i: (0, i))],
        out_shape=[arr_t, arr, arr, arr_t, jax.ShapeDtypeStruct((PEER_HEADS, S), F32)],
        compiler_params=_cparams("parallel"),
        name="peer_route",
    )(q_slots, sub_keys)


def _peer_expert_kernel(x_ref, u_ref, v_ref, s0a_ref, s0b_ref, ca_ref, cb_ref, s1_ref, e1_ref, tau_ref,
                        y_ref, ha_scr, hb_scr, wa_scr, wb_scr):
    k = pl.program_id(1)
    te, tq = ha_scr.shape
    n_rows = te // PEER_N_KEYS
    D = y_ref.shape[1]
    nc = D // n_rows

    @pl.when(k == 0)
    def _():
        y_ref[...] = jnp.zeros_like(y_ref)
        hb_scr[...] = jnp.zeros_like(hb_scr)
        wa_scr[...] = jnp.zeros_like(wa_scr)

    def phase(u_lo, h_out, w_prev, h_prev, w_out, s0_ref, c_ref):
        rb = 32

        def y_chunk(r):
            cols = pl.ds(r * nc, nc)
            y_ref[:, cols] += lax.dot_general(w_prev[...], v_ref[pl.ds(u_lo, te), cols], _TN,
                                              preferred_element_type=F32)

        def h_half(n):
            toks = pl.ds(n * (tq // 2), tq // 2)
            h_out[:, toks] = lax.dot_general(u_ref[pl.ds(u_lo, te), :], x_ref[toks, :], _NT,
                                             preferred_element_type=F32)

        def gate_block(tc, sb):
            tl = pl.ds(tc * LANE, LANE)
            keys = pl.ds(sb * rb, rb)
            gs = [None] * n_rows
            for h in range(PEER_HEADS):
                s1b = s1_ref[h, keys, tl]
                e1b = e1_ref[h, keys, tl]
                taub = tau_ref[pl.ds(h, 1), tl]
                for r in range(n_rows):
                    ssum = s0_ref[h, pl.ds(r, 1), tl] + s1b
                    t = jnp.where(ssum >= taub, e1b, 0.0) * c_ref[h, pl.ds(r, 1), tl]
                    gs[r] = t if gs[r] is None else gs[r] + t
            for r in range(n_rows):
                rows = pl.ds(r * PEER_N_KEYS + sb * rb, rb)
                hp = h_prev[rows, tl]
                act = 0.5 * hp * (1.0 + lax.erf(hp * math.sqrt(0.5)))
                w_out[rows, tl] = (gs[r] * act).astype(w_out.dtype)

        blocks = [(tc, sb) for tc in range(tq // LANE) for sb in range(PEER_N_KEYS // rb)]
        pieces = [functools.partial(y_chunk, r) for r in range(n_rows)] + [functools.partial(h_half, n) for n in range(2)]
        weights = [1] * n_rows + [2, 2]
        per = len(blocks) // sum(weights)
        b0 = 0
        for piece, wgt in zip(pieces, weights):
            piece()
            for tc, sb in blocks[b0:b0 + per * wgt]:
                gate_block(tc, sb)
            b0 += per * wgt
        for tc, sb in blocks[b0:]:
            gate_block(tc, sb)

    phase(0, ha_scr, wa_scr, hb_scr, wb_scr, s0a_ref, ca_ref)
    phase(te, hb_scr, wb_scr, ha_scr, wa_scr, s0b_ref, cb_ref)


def peer_experts(x_bf, u_bf, v_bf, s0_t, s1, e1, c_t, tau, *, tq, te):
    S, D = x_bf.shape
    E = u_bf.shape[0]
    n_tiles = E // te
    n_blocks = n_tiles // 2
    n_rows = te // PEER_N_KEYS
    assert s0_t.shape == c_t.shape == (PEER_HEADS, n_tiles, n_rows, S)
    rt = pl.BlockSpec((PEER_HEADS, PEER_N_KEYS, tq), lambda i, k: (0, 0, i))
    row_a = pl.BlockSpec((PEER_HEADS, None, n_rows, tq), lambda i, k: (0, jnp.maximum(2 * k - 1, 0), 0, i))
    row_b = pl.BlockSpec((PEER_HEADS, None, n_rows, tq),
                         lambda i, k: (0, jnp.minimum(2 * k, n_tiles - 1), 0, i))
    return pl.pallas_call(
        _peer_expert_kernel,
        grid=(S // tq, n_blocks + 1),
        in_specs=[pl.BlockSpec((tq, D), lambda i, k: (i, 0)),
                  pl.BlockSpec((2 * te, D), lambda i, k: (jnp.minimum(k, n_blocks - 1), 0)),
                  pl.BlockSpec((2 * te, D), lambda i, k: (jnp.maximum(k - 1, 0), 0)),
                  row_a, row_b, row_a, row_b, rt, rt,
                  pl.BlockSpec((PEER_HEADS, tq), lambda i, k: (0, i))],
        out_specs=pl.BlockSpec((tq, D), lambda i, k: (i, 0)),
        out_shape=jax.ShapeDtypeStruct((S, D), F32),
        scratch_shapes=[pltpu.VMEM((te, tq), F32), pltpu.VMEM((te, tq), F32),
                        pltpu.VMEM((te, tq), BF16), pltpu.VMEM((te, tq), BF16)],
        compiler_params=_cparams("parallel", "arbitrary"),
        name="peer_experts",
    )(x_bf, u_bf, v_bf, s0_t, s0_t, c_t, c_t, s1, e1, tau)


def _pad_cols(w, width):
    return jnp.pad(w, ((0, 0), (0, width - w.shape[1])))


def _rot_cols(w):
    h = w.shape[1] // 2
    return jnp.concatenate([-w[:, h:], w[:, :h]], axis=1)


def _split_w_in(w):
    widths = (512, 512, 512, 512, 128, 128, 512, 512, 512, D_Q_LORA, D_KV_LORA, D_ROPE)
    out, o = [], 0
    for n in widths:
        out.append(w[:, o:o + n])
        o += n
    return out


def _layer_weights(w_in, a_scale, b_scale, c_scale):
    (a_q, a_k, a_v, b_q, b_k, b_v, c_q, c_k, c_v, d_cq, d_ckv, d_kr) = _split_w_in(w_in)
    b_k0, b_k1 = b_k[:, :64], b_k[:, 64:]
    b_v0, b_v1 = b_v[:, :64], b_v[:, 64:]
    w_abc = jnp.concatenate(
        [a_q, a_k, a_v,
         b_q, _pad_cols(b_k0, LANE), _pad_cols(b_k1, LANE), _pad_cols(b_v0, LANE), _pad_cols(b_v1, LANE),
         c_q, c_k, c_v], axis=1).astype(BF16)
    ones = lambda n: jnp.ones((n,), F32)
    s_abc = jnp.concatenate([ones(512) * a_scale, ones(1024),
                             ones(512) * b_scale, ones(512),
                             ones(512) * c_scale, ones(1024)])[None, :]
    w_d = jnp.concatenate([d_cq, d_ckv, _pad_cols(d_kr, LANE), _pad_cols(_rot_cols(d_kr), LANE)],
                          axis=1).astype(BF16)
    return w_abc, s_abc, w_d


def _mla_weights(w_uq, w_ukv):
    wq = w_uq.reshape(D_Q_LORA, D_HEADS, D_NOPE + D_ROPE)
    nope, pe = wq[..., :D_NOPE], wq[..., D_NOPE:]
    pe_rot = jnp.concatenate([-pe[..., D_ROPE // 2:], pe[..., :D_ROPE // 2]], axis=-1)
    z64 = jnp.zeros((D_Q_LORA, D_HEADS, LANE - D_ROPE), F32)
    z128 = jnp.zeros((D_Q_LORA, D_HEADS, D_NOPE), F32)
    wqa = jnp.concatenate([nope, pe, z64], axis=-1).reshape(D_Q_LORA, -1).astype(BF16)
    wqb = jnp.concatenate([z128, pe_rot, z64], axis=-1).reshape(D_Q_LORA, -1).astype(BF16)
    wkv = w_ukv.reshape(D_KV_LORA, D_HEADS, D_NOPE + D_V)
    wkv = jnp.concatenate([wkv[..., :D_NOPE].reshape(D_KV_LORA, -1),
                           wkv[..., D_NOPE:].reshape(D_KV_LORA, -1)], axis=1).astype(BF16)
    return wqa, wqb, wkv


def _rope_tables(positions):
    half = D_ROPE // 2
    inv = ROPE_THETA ** (-jnp.arange(half, dtype=F32) / half)
    ang = positions.astype(F32)[..., None] * inv
    cos, sin = jnp.cos(ang), jnp.sin(ang)
    S = positions.shape[0]
    one = jnp.ones((S, D_NOPE), F32)
    z = jnp.zeros((S, LANE - D_ROPE), F32)
    cosm = jnp.concatenate([one, cos, cos, z], axis=1)
    sinm = jnp.concatenate([jnp.zeros((S, D_NOPE), F32), sin, sin, z], axis=1)
    return cosm, sinm


def kernel(x, positions, t5_table, w_in, w_gate, b_gate, w_branch, w_out, swa_sinks, diff_lambda_q1, diff_lambda_k1, diff_lambda_q2, diff_lambda_k2, diff_subln_g, mla_q_norm_g, mla_kv_norm_g, mla_w_uq, mla_w_ukv, ln1_g, ln1_b, peer_w_q, peer_sub_keys, peer_u, peer_v, ln2_g, ln2_b):
    B, S, D = x.shape
    assert B == 1 and D == D_MODEL and S % MOBA_BLOCK == 0
    depth = w_in.shape[0]
    alpha = (2.0 * depth) ** 0.25
    tm = min(512, S)
    T = min(512, S)

    tab_a = t5_table[:, :A_HEADS]
    tab_b = t5_table[:, A_HEADS:A_HEADS + B_HEADS]
    tab_c = t5_table[:, A_HEADS + B_HEADS:]
    bias_a = t5_tiles(tab_a, T=T, n_near=_n_near(T) + 1)[:, None]
    bias_c = t5_tiles(tab_c, T=T, n_near=_n_near(T) + 1)[:, None]
    band_b = t5_band(tab_b, tq=tm, window=SWA_WINDOW)
    cosm, sinm = _rope_tables(positions[0])

    xf = x[0]
    xb = xf.astype(BF16)
    for layer in range(depth):
        lam_init = 0.8 - 0.6 * math.exp(-0.3 * layer)
        w_abc, s_abc, w_d = _layer_weights(w_in[layer], 128 ** -0.5 * LOG2E, B_HEAD_DIM ** -0.5 * LOG2E,
                                           C_HEAD_DIM ** -0.5 * LOG2E)
        hs = proj_slots(xb, w_abc, s_abc, BF16, tm=tm, slots_per_step=8)
        hd = proj_slots(xb, w_d, jnp.ones((1, w_d.shape[1]), F32), F32, tm=tm, slots_per_step=8)

        nb = S // MOBA_BLOCK
        nbp = -(-nb // LANE) * LANE
        kmean = jnp.pad(moba_kmean(hs, 4, S), ((0, 0), (0, nbp - nb), (0, 0)))
        selb = moba_select(hs, 0, kmean, S, tq=min(1024, S))
        o_a = flash_pipe("A", hs, hs, hs, S=S, T=T, q_slot=lambda u: u, k_slot=lambda u: 4 + u,
                    v_slot=lambda u: 8 + u, bias=bias_a, selb=selb)
        sinks = ((swa_sinks[layer] - tab_b[T5_BUCKETS - 1]) * LOG2E).reshape(B_HEADS // 2, 2)
        o_b = swa_attention(hs, band_b, sinks, S=S, tq=tm, q_slot0=12, k_slot0=16, v_slot0=18)
        lam_params = jnp.stack([diff_lambda_q1[layer], diff_lambda_k1[layer],
                                diff_lambda_q2[layer], diff_lambda_k2[layer]])
        o_c = flash_pipe("C", hs, hs, hs, S=S, T=T, q_slot=lambda u: 20 + u, k_slot=lambda u: 24 + u,
                    v_slot=lambda u: 28 + u, bias=bias_c, lam_params=lam_params,
                    subln_g=diff_subln_g[layer][None, :], lam_init=lam_init)
        wqa, wqb, wkv = _mla_weights(mla_w_uq[layer], mla_w_ukv[layer])
        q_d, k_d, v_d = mla_prep(hd, mla_q_norm_g[layer][None, :], mla_kv_norm_g[layer][None, :],
                                 wqa, wqb, wkv, cosm, sinm, S=S, tm=tm)
        o_d = flash_pipe("D", q_d, k_d, v_d, S=S, T=T, q_slot=lambda u: u, k_slot=lambda u: u,
                    v_slot=lambda u: u)

        merged = merge(xb, (o_a, o_b, o_c, o_d),
                       w_gate[layer].astype(BF16), b_gate[layer][:, None, :], w_branch[layer].astype(BF16), tm=tm, tn=512)
        xf, xb = out_ln(xf, merged, w_out[layer].astype(BF16), ln1_g[layer][None, :],
                        ln1_b[layer][None, :], alpha=alpha, tm=tm)

        q_slots = proj_slots(xb, peer_w_q[layer].astype(BF16), jnp.ones((1, D), F32), F32,
                             tm=tm, slots_per_step=8)
        te = 512
        s0_t, s1, e1, c_t, tau = peer_route(q_slots, peer_sub_keys[layer], S=S, tq=tm,
                                            n_rows=te // PEER_N_KEYS)
        y = peer_experts(xb, peer_u[layer].astype(BF16), peer_v[layer].astype(BF16),
                         s0_t, s1, e1, c_t, tau, tq=tm, te=te)
        xf, xb = add_ln(xf, y, ln2_g[layer][None, :], ln2_b[layer][None, :], alpha=alpha, tm=tm)
    return xf[None]
```

```python
import functools
import math

import jax
import jax.numpy as jnp
from jax import lax
from jax.experimental import pallas as pl
from jax.experimental.pallas import tpu as pltpu

F32 = jnp.float32
BF16 = jnp.bfloat16
LANE = 128
NEG = -1e30
LOG2E = math.log2(math.e)

D_MODEL = 2048
A_HEADS = 4
MOBA_BLOCK = 256
MOBA_TOPK = 3
B_HEADS = 8
B_KV_HEADS = 2
B_HEAD_DIM = 64
SWA_WINDOW = 128
C_HEADS = 4
C_HEAD_DIM = 64
D_HEADS = 4
D_Q_LORA = 512
D_KV_LORA = 256
D_NOPE = 128
D_ROPE = 64
D_V = 128
ROPE_THETA = 10000.0
N_BRANCHES = 4
BRANCH_W = 512
T5_BUCKETS = 32
T5_MAX_EXACT = T5_BUCKETS // 2
T5_MAX_DISTANCE = 1024
PEER_HEADS = 8
PEER_N_KEYS = 128
PEER_D_KEY = 128
PEER_TOPK = 16
LN_EPS = 1e-5
RMS_EPS = 1e-6

VMEM_LIMIT_MB = 56


def _cparams(*sem):
    return pltpu.CompilerParams(dimension_semantics=sem,
                                vmem_limit_bytes=VMEM_LIMIT_MB * 1024 * 1024)


_NT = (((1,), (1,)), ((), ()))
_TN = (((0,), (0,)), ((), ()))


def _proj_kernel(x_ref, w_ref, s_ref, o_ref):
    acc = jnp.dot(x_ref[...], w_ref[...], preferred_element_type=F32) * s_ref[...]
    for c in range(o_ref.shape[0]):
        o_ref[c] = acc[:, c * LANE:(c + 1) * LANE].astype(o_ref.dtype)


def proj_slots(x_bf, w_bf, colscale, out_dtype, *, tm, slots_per_step):
    S, K = x_bf.shape
    N = w_bf.shape[1]
    tn = slots_per_step * LANE
    return pl.pallas_call(
        _proj_kernel,
        grid=(N // tn, S // tm),
        in_specs=[pl.BlockSpec((tm, K), lambda j, i: (i, 0)),
                  pl.BlockSpec((K, tn), lambda j, i: (0, j)),
                  pl.BlockSpec((1, tn), lambda j, i: (0, j))],
        out_specs=pl.BlockSpec((slots_per_step, tm, LANE), lambda j, i: (j, i, 0)),
        out_shape=jax.ShapeDtypeStruct((N // LANE, S, LANE), out_dtype),
        compiler_params=_cparams("parallel", "parallel"),
        name="proj_slots",
    )(x_bf, w_bf, colscale)


def _t5_tiles_kernel(tab_ref, o_ref, *, T, n_heads):
    dj = pl.program_id(0)
    row = lax.broadcasted_iota(jnp.int32, (T, T), 0)
    col = lax.broadcasted_iota(jnp.int32, (T, T), 1)
    bucket = _t5_bucket(jnp.maximum(row - col + dj * T, 0))
    for h in range(n_heads):
        last = tab_ref[T5_BUCKETS - 1, h]
        acc = jnp.zeros((T, T), F32)
        for b in range(T5_BUCKETS - 1):
            acc = jnp.where(bucket == b, (tab_ref[b, h] - last) * LOG2E, acc)
        o_ref[h, 0] = acc


def t5_tiles(tab, *, T, n_near):
    H = tab.shape[1]
    return pl.pallas_call(
        functools.partial(_t5_tiles_kernel, T=T, n_heads=H),
        grid=(n_near,),
        in_specs=[pl.BlockSpec(memory_space=pltpu.SMEM)],
        out_specs=pl.BlockSpec((H, 1, T, T), lambda d: (0, d, 0, 0)),
        out_shape=jax.ShapeDtypeStruct((H, n_near, T, T), F32),
        compiler_params=_cparams("parallel"),
        name="t5_tiles",
    )(tab)


def _n_near(T):
    return (T5_MAX_DISTANCE - 1 + T - 1) // T + 1


def _t5_bucket(n):
    nf = jnp.maximum(n, 1).astype(F32)
    large = T5_MAX_EXACT + (jnp.log(nf / T5_MAX_EXACT) / math.log(T5_MAX_DISTANCE / T5_MAX_EXACT)
                            * (T5_BUCKETS - T5_MAX_EXACT)).astype(jnp.int32)
    return jnp.where(n < T5_MAX_EXACT, n, jnp.minimum(large, T5_BUCKETS - 1))


def _t5_band_kernel(tab_ref, o_ref, *, window):
    h = pl.program_id(0)
    rows, cols = o_ref.shape
    dist = (lax.broadcasted_iota(jnp.int32, (rows, cols), 0)
            - lax.broadcasted_iota(jnp.int32, (rows, cols), 1) + window)
    bucket = _t5_bucket(jnp.maximum(dist, 0))
    last = tab_ref[T5_BUCKETS - 1, h]
    acc = jnp.zeros((rows, cols), F32)
    for b in range(T5_BUCKETS - 1):
        acc = jnp.where(bucket == b, (tab_ref[b, h] - last) * LOG2E, acc)
    o_ref[...] = jnp.where((dist >= 0) & (dist < window), acc, NEG)


def t5_band(tab, *, tq, window):
    H = tab.shape[1]
    cols = tq + 2 * window
    return pl.pallas_call(
        functools.partial(_t5_band_kernel, window=window),
        grid=(H,),
        in_specs=[pl.BlockSpec(memory_space=pltpu.SMEM)],
        out_specs=pl.BlockSpec((None, tq, cols), lambda h: (h, 0, 0)),
        out_shape=jax.ShapeDtypeStruct((H, tq, cols), F32),
        compiler_params=_cparams("parallel"),
        name="t5_band",
    )(tab)


def _swa_kernel(q_ref, k_ref, v_ref, band_ref, sink_ref, o_ref, *, window):
    u = pl.program_id(0)
    qi = pl.program_id(1)
    tq = q_ref.shape[0]
    nk = tq + window
    half = LANE // 2
    start = pl.multiple_of(jnp.maximum(qi * tq - window, 0), window)
    kb = k_ref[pl.ds(start, nk), :]
    vb = v_ref[pl.ds(start, nk), :]
    q = q_ref[...]
    z = jnp.zeros((tq, half), q.dtype)
    outs = []
    for c in range(2):
        qc = jnp.concatenate([q[:, c * half:(c + 1) * half], z], axis=1)
        band = jnp.where(qi == 0, band_ref[c, :, window:], band_ref[c, :, :nk])
        s = lax.dot_general(qc, kb, _NT, preferred_element_type=F32) + band
        sink = sink_ref[u, c]
        m = jnp.maximum(jnp.max(s, axis=-1, keepdims=True), sink)
        p = jnp.exp2(s - m)
        l = jnp.sum(p, axis=-1, keepdims=True) + jnp.exp2(sink - m)
        o = jnp.dot(p.astype(vb.dtype), vb, preferred_element_type=F32) / l
        outs.append(o[:, :half])
    o_ref[...] = jnp.concatenate(outs, axis=1).astype(o_ref.dtype)


def swa_attention(hs, band, sinks, *, S, tq, q_slot0, k_slot0, v_slot0):
    n_units = B_HEADS // 2
    per_kv = n_units // B_KV_HEADS
    cols = band.shape[-1]
    return pl.pallas_call(
        functools.partial(_swa_kernel, window=SWA_WINDOW),
        grid=(n_units, S // tq),
        in_specs=[pl.BlockSpec((None, tq, LANE), lambda u, i: (q_slot0 + u, i, 0)),
                  pl.BlockSpec((None, S, LANE), lambda u, i: (k_slot0 + u // per_kv, 0, 0)),
                  pl.BlockSpec((None, S, LANE), lambda u, i: (v_slot0 + u // per_kv, 0, 0)),
                  pl.BlockSpec((2, tq, cols), lambda u, i: (u, 0, 0)),
                  pl.BlockSpec(memory_space=pltpu.SMEM)],
        out_specs=pl.BlockSpec((None, tq, LANE), lambda u, i: (u, i, 0)),
        out_shape=jax.ShapeDtypeStruct((n_units, S, LANE), BF16),
        compiler_params=_cparams("parallel", "arbitrary"),
        name="swa",
    )(hs, hs, hs, band, sinks)


def _kmean_kernel(k_ref, o_ref, *, blk):
    k = k_ref[...].astype(F32)
    nb = k.shape[0] // blk
    o_ref[...] = k.reshape(nb, blk, LANE).sum(axis=1) * (1.0 / blk)


def moba_kmean(hs, k_slot0, S):
    nb = S // MOBA_BLOCK
    return pl.pallas_call(
        functools.partial(_kmean_kernel, blk=MOBA_BLOCK),
        grid=(A_HEADS,),
        in_specs=[pl.BlockSpec((None, S, LANE), lambda h: (k_slot0 + h, 0, 0))],
        out_specs=pl.BlockSpec((None, nb, LANE), lambda h: (h, 0, 0)),
        out_shape=jax.ShapeDtypeStruct((A_HEADS, nb, LANE), F32),
        compiler_params=_cparams("parallel"),
        name="moba_kmean",
    )(hs)


def _moba_select_kernel(q_ref, km_ref, o_ref, *, topk):
    tq = q_ref.shape[0]
    q = q_ref[...].astype(F32)
    gate = lax.dot_general(q, km_ref[...], _NT, precision=lax.Precision.HIGHEST,
                           preferred_element_type=F32)
    lane = lax.broadcasted_iota(jnp.int32, gate.shape, 1)
    row = lax.broadcasted_iota(jnp.int32, gate.shape, 0)
    own = pl.program_id(1) * (tq // MOBA_BLOCK) + row // MOBA_BLOCK
    nbp = gate.shape[1]
    gate = jnp.where(lane < own, gate, -jnp.inf)
    mask = jnp.where(lane == own, 0.0, NEG)
    for _ in range(topk):
        mx = jnp.max(gate, axis=-1, keepdims=True)
        idx = jnp.min(jnp.where(gate == mx, lane, nbp), axis=-1, keepdims=True)
        hit = lane == idx
        mask = jnp.where(hit, jnp.where(mx > -jnp.inf, 0.0, mask), mask)
        gate = jnp.where(hit, -jnp.inf, gate)
    o_ref[...] = mask.astype(o_ref.dtype)


def moba_select(hs, q_slot0, kmean_pad, S, *, tq):
    nbp = kmean_pad.shape[1]
    return pl.pallas_call(
        functools.partial(_moba_select_kernel, topk=MOBA_TOPK),
        grid=(A_HEADS, S // tq),
        in_specs=[pl.BlockSpec((None, tq, LANE), lambda h, i: (q_slot0 + h, i, 0)),
                  pl.BlockSpec((None, nbp, LANE), lambda h, i: (h, 0, 0))],
        out_specs=pl.BlockSpec((None, tq, nbp), lambda h, i: (h, i, 0)),
        out_shape=jax.ShapeDtypeStruct((A_HEADS, S, nbp), BF16),
        compiler_params=_cparams("parallel", "parallel"),
        name="moba_select",
    )(hs, kmean_pad)


def _flash_pipe_kernel(*refs, mode, T, n_near, lam_init):
    it = iter(refs)
    q_ref, k_ref, v_ref = next(it), next(it), next(it)
    bias_ref = next(it) if mode in "AC" else None
    selb_ref = next(it) if mode == "A" else None
    lam_ref, g_ref = (next(it), next(it)) if mode == "C" else (None, None)
    o_ref = next(it)
    m_scr, l_scr, acc_scr = next(it), next(it), next(it)
    rest = list(it)
    s_scr, p_scr = rest[:len(rest) // 2], rest[len(rest) // 2:]

    qi = pl.program_id(1)
    q = q_ref[...]
    half = LANE // 2
    if mode == "A":
        qs = [jnp.concatenate([q, selb_ref[...]], axis=1)]
    elif mode == "C":
        lane = lax.broadcasted_iota(jnp.int32, q.shape, 1)
        zq = jnp.zeros_like(q)
        qs = [jnp.where(lane < half, q, zq), jnp.where(lane >= half, q, zq)]
    else:
        qs = [q]
    n_sub = len(qs)

    def k_tile(j):
        kt = k_ref[pl.ds(pl.multiple_of(j * T, T), T), :]
        if mode == "A":
            n_blk = selb_ref.shape[-1]
            blk = lax.broadcasted_iota(jnp.int32, (T, n_blk), 1)
            row = lax.broadcasted_iota(jnp.int32, (T, n_blk), 0)
            own = j * (T // MOBA_BLOCK) + row // MOBA_BLOCK
            kt = jnp.concatenate([kt, jnp.where(blk == own, 1.0, 0.0).astype(kt.dtype)], axis=1)
        return kt

    def v_tile(j):
        return v_ref[pl.ds(pl.multiple_of(j * T, T), T), :]

    def scores(c, kt):
        return lax.dot_general(qs[c], kt, _NT, preferred_element_type=F32)

    kt0 = k_tile(0)
    for c in range(n_sub):
        s_scr[c][0] = scores(c, kt0)
        p_scr[c][1] = jnp.zeros((T, T), p_scr[c].dtype)
        m_scr[c] = jnp.full((T, LANE), NEG, F32)
        l_scr[c] = jnp.zeros((T, LANE), F32)
        acc_scr[c] = jnp.zeros(acc_scr.shape[1:], F32)

    def step(j, slot, near):
        nslot = 1 - slot
        kt_next = k_tile(jnp.minimum(j + 1, qi))
        vt_prev = v_tile(jnp.maximum(j - 1, 0))
        if near:
            dj = qi - j
            row = lax.broadcasted_iota(jnp.int32, (T, T), 0)
            col = lax.broadcasted_iota(jnp.int32, (T, T), 1)
            valid = row - col + dj * T >= 0
        for c in range(n_sub):
            pv = jnp.dot(p_scr[c][nslot], vt_prev, preferred_element_type=F32)
            s = s_scr[c][slot]
            s_scr[c][nslot] = scores(c, kt_next)
            if near:
                if bias_ref is not None:
                    s = s + bias_ref[0, dj]
                s = jnp.where(valid, s, NEG)
            m_prev = m_scr[c]
            m_new = jnp.maximum(m_prev, jnp.max(s, axis=-1, keepdims=True))
            alpha = jnp.exp2(m_prev - m_new)
            ps = [jnp.exp2(s[:, i * LANE:(i + 1) * LANE] - m_new) for i in range(T // LANE)]
            p_scr[c][slot] = jnp.concatenate(ps, axis=1).astype(p_scr[c].dtype)
            l_scr[c] = alpha * l_scr[c] + functools.reduce(lambda a, b: a + b, ps)
            acc_scr[c] = alpha * (acc_scr[c] + pv)
            m_scr[c] = m_new

    n_pairs = jnp.maximum(qi - n_near + 1, 0) // 2

    def far_pair(i, _):
        step(2 * i, 0, False)
        step(2 * i + 1, 1, False)
        return 0

    lax.fori_loop(0, n_pairs, far_pair, 0)
    for t in range(n_near + 1):
        jn = 2 * n_pairs + t

        @pl.when(jn <= qi)
        def _():
            step(jn, t % 2, True)

    last = lax.rem(qi, 2)
    vt = v_tile(qi)
    outs = []
    for c in range(n_sub):
        acc = acc_scr[c] + jnp.dot(p_scr[c][last], vt, preferred_element_type=F32)
        outs.append(acc / jnp.sum(l_scr[c], axis=-1, keepdims=True))
    if mode == "C":
        lp = lam_ref[...]
        lam = (jnp.exp(jnp.sum(lp[0:1] * lp[1:2], axis=-1, keepdims=True))
               - jnp.exp(jnp.sum(lp[2:3] * lp[3:4], axis=-1, keepdims=True)) + lam_init)
        o = outs[0] - lam * outs[1]
        o = o * lax.rsqrt(jnp.mean(o * o, axis=-1, keepdims=True) + RMS_EPS) * g_ref[...]
        o = o * (1.0 - lam_init)
    else:
        o = outs[0]
    o_ref[...] = o.astype(o_ref.dtype)


def flash_pipe(mode, q_arr, k_arr, v_arr, *, S, T, q_slot, k_slot, v_slot, n_units=4, bias=None,
               selb=None, lam_params=None, subln_g=None, lam_init=0.0):
    dq = q_arr.shape[-1]
    dk = k_arr.shape[-1]
    dv = v_arr.shape[-1]
    n_near = 1 if mode == "D" else _n_near(T)
    n_sub = 2 if mode == "C" else 1
    in_specs = [pl.BlockSpec((None, T, dq), lambda u, i: (q_slot(u), i, 0)),
                pl.BlockSpec((None, S, dk), lambda u, i: (k_slot(u), 0, 0)),
                pl.BlockSpec((None, S, dv), lambda u, i: (v_slot(u), 0, 0))]
    args = [q_arr, k_arr, v_arr]
    if bias is not None:
        in_specs.append(pl.BlockSpec((None,) + bias.shape[1:], lambda u, i: (u, 0, 0, 0, 0)))
        args.append(bias)
    if selb is not None:
        in_specs.append(pl.BlockSpec((None, T, selb.shape[-1]), lambda u, i: (u, i, 0)))
        args.append(selb)
    if lam_params is not None:
        in_specs.append(pl.BlockSpec(lam_params.shape, lambda u, i: (0, 0)))
        in_specs.append(pl.BlockSpec(subln_g.shape, lambda u, i: (0, 0)))
        args += [lam_params, subln_g]
    return pl.pallas_call(
        functools.partial(_flash_pipe_kernel, mode=mode, T=T, n_near=n_near, lam_init=lam_init),
        grid=(n_units, S // T),
        in_specs=in_specs,
        out_specs=pl.BlockSpec((None, T, LANE), lambda u, i: (u, i, 0)),
        out_shape=jax.ShapeDtypeStruct((n_units, S, LANE), BF16),
        scratch_shapes=([pltpu.VMEM((n_sub, T, LANE), F32), pltpu.VMEM((n_sub, T, LANE), F32),
                         pltpu.VMEM((n_sub, T, dv), F32)]
                        + [pltpu.VMEM((2, T, T), F32)] * n_sub + [pltpu.VMEM((2, T, T), BF16)] * n_sub),
        compiler_params=_cparams("parallel", "arbitrary"),
        name="flashp_" + mode,
    )(*args)


def _rms(x, g):
    return x * lax.rsqrt(jnp.mean(x * x, axis=-1, keepdims=True) + RMS_EPS) * g


def _mla_prep_kernel(hd_ref, gq_ref, gkv_ref, wqa_ref, wqb_ref, wkv_ref, cos_ref, sin_ref,
                     qo_ref, ko_ref, vo_ref, *, scale):
    cosm = cos_ref[...]
    sinm = sin_ref[...]
    cq = jnp.concatenate([hd_ref[c] for c in range(4)], axis=1)
    cqn = _rms(cq, gq_ref[...]).astype(BF16)
    qa = jnp.dot(cqn, wqa_ref[...], preferred_element_type=F32)
    qb = jnp.dot(cqn, wqb_ref[...], preferred_element_type=F32)
    w = 2 * LANE
    for h in range(D_HEADS):
        qe = (qa[:, h * w:(h + 1) * w] * cosm + qb[:, h * w:(h + 1) * w] * sinm) * scale
        qo_ref[h] = qe.astype(qo_ref.dtype)
    ckv = jnp.concatenate([hd_ref[4], hd_ref[5]], axis=1)
    ckvn = _rms(ckv, gkv_ref[...]).astype(BF16)
    kv = jnp.dot(ckvn, wkv_ref[...], preferred_element_type=F32)
    kpe = hd_ref[6] * cosm[:, LANE:] + hd_ref[7] * sinm[:, LANE:]
    for h in range(D_HEADS):
        ko_ref[h] = jnp.concatenate([kv[:, h * LANE:(h + 1) * LANE], kpe], axis=1).astype(ko_ref.dtype)
        vo_ref[h] = kv[:, (D_HEADS + h) * LANE:(D_HEADS + h + 1) * LANE].astype(vo_ref.dtype)


def mla_prep(hd, gq, gkv, wqa, wqb, wkv, cosm, sinm, *, S, tm):
    w = 2 * LANE
    full = lambda a: pl.BlockSpec(a.shape, lambda i: (0,) * a.ndim)
    return pl.pallas_call(
        functools.partial(_mla_prep_kernel, scale=(D_NOPE + D_ROPE) ** -0.5 * LOG2E),
        grid=(S // tm,),
        in_specs=[pl.BlockSpec((8, tm, LANE), lambda i: (0, i, 0)),
                  full(gq), full(gkv), full(wqa), full(wqb), full(wkv),
                  pl.BlockSpec((tm, w), lambda i: (i, 0)),
                  pl.BlockSpec((tm, w), lambda i: (i, 0))],
        out_specs=[pl.BlockSpec((D_HEADS, tm, w), lambda i: (0, i, 0)),
                   pl.BlockSpec((D_HEADS, tm, w), lambda i: (0, i, 0)),
                   pl.BlockSpec((D_HEADS, tm, LANE), lambda i: (0, i, 0))],
        out_shape=[jax.ShapeDtypeStruct((D_HEADS, S, w), BF16),
                   jax.ShapeDtypeStruct((D_HEADS, S, w), BF16),
                   jax.ShapeDtypeStruct((D_HEADS, S, LANE), BF16)],
        compiler_params=_cparams("parallel"),
        name="mla_prep",
    )(hd, gq, gkv, wqa, wqb, wkv, cosm, sinm)


def _merge_kernel(x_ref, oa_ref, ob_ref, oc_ref, od_ref, wg_ref, bg_ref, wbr_ref, o_ref):
    x = x_ref[...]
    acc = None
    for n, o in enumerate((oa_ref, ob_ref, oc_ref, od_ref)):
        g = jnp.dot(x, wg_ref[n], preferred_element_type=F32) + bg_ref[n]
        ocat = jnp.concatenate([o[c] for c in range(4)], axis=1)
        b = jnp.dot(ocat, wbr_ref[n], preferred_element_type=F32)
        t = jax.nn.sigmoid(g) * b
        acc = t if acc is None else acc + t
    o_ref[...] = acc.astype(o_ref.dtype)


def merge(x_bf, outs, wg, bg, wbr, *, tm, tn):
    S, D = x_bf.shape
    o_spec = pl.BlockSpec((4, tm, LANE), lambda j, i: (0, i, 0))
    return pl.pallas_call(
        _merge_kernel,
        grid=(D // tn, S // tm),
        in_specs=[pl.BlockSpec((tm, D), lambda j, i: (i, 0)), o_spec, o_spec, o_spec, o_spec,
                  pl.BlockSpec((N_BRANCHES, D, tn), lambda j, i: (0, 0, j)),
                  pl.BlockSpec((N_BRANCHES, 1, tn), lambda j, i: (0, 0, j)),
                  pl.BlockSpec((N_BRANCHES, BRANCH_W, tn), lambda j, i: (0, 0, j))],
        out_specs=pl.BlockSpec((tm, tn), lambda j, i: (i, j)),
        out_shape=jax.ShapeDtypeStruct((S, D), BF16),
        compiler_params=_cparams("parallel", "parallel"),
        name="merge",
    )(x_bf, *outs, wg, bg, wbr)


def _layer_norm(y, g, b):
    mu = jnp.mean(y, axis=-1, keepdims=True)
    var = jnp.mean(jnp.square(y - mu), axis=-1, keepdims=True)
    return (y - mu) * lax.rsqrt(var + LN_EPS) * g + b


def _out_ln_kernel(x_ref, m_ref, w_ref, g_ref, b_ref, of_ref, ob_ref, *, alpha):
    y = alpha * x_ref[...] + jnp.dot(m_ref[...], w_ref[...], preferred_element_type=F32)
    o = _layer_norm(y, g_ref[...], b_ref[...])
    of_ref[...] = o
    ob_ref[...] = o.astype(ob_ref.dtype)


def out_ln(x, merged, w_out, g, b, *, alpha, tm):
    S, D = x.shape
    row = pl.BlockSpec((tm, D), lambda i: (i, 0))
    vec = pl.BlockSpec((1, D), lambda i: (0, 0))
    return pl.pallas_call(
        functools.partial(_out_ln_kernel, alpha=alpha),
        grid=(S // tm,),
        in_specs=[row, row, pl.BlockSpec((D, D), lambda i: (0, 0)), vec, vec],
        out_specs=[row, row],
        out_shape=[jax.ShapeDtypeStruct((S, D), F32), jax.ShapeDtypeStruct((S, D), BF16)],
        compiler_params=_cparams("parallel"),
        name="out_ln",
    )(x, merged, w_out, g, b)


def _add_ln_kernel(x_ref, y_ref, g_ref, b_ref, of_ref, ob_ref, *, alpha):
    o = _layer_norm(alpha * x_ref[...] + y_ref[...], g_ref[...], b_ref[...])
    of_ref[...] = o
    ob_ref[...] = o.astype(ob_ref.dtype)


def add_ln(x, y, g, b, *, alpha, tm):
    S, D = x.shape
    row = pl.BlockSpec((tm, D), lambda i: (i, 0))
    vec = pl.BlockSpec((1, D), lambda i: (0, 0))
    return pl.pallas_call(
        functools.partial(_add_ln_kernel, alpha=alpha),
        grid=(S // tm,),
        in_specs=[row, row, vec, vec],
        out_specs=[row, row],
        out_shape=[jax.ShapeDtypeStruct((S, D), F32), jax.ShapeDtypeStruct((S, D), BF16)],
        compiler_params=_cparams("parallel"),
        name="add_ln",
    )(x, y, g, b)


def _top_sorted(s, k):
    vals = []
    cur = s
    for r in range(k):
        m = jnp.max(cur, axis=0, keepdims=True)
        vals.append(m)
        if r + 1 < k:
            cur = jnp.where(cur == m, -jnp.inf, cur)
    return vals


def _peer_route_kernel(q_ref, keys_ref, th_ref, s1_ref, e1_ref, c_ref):
    K = PEER_TOPK

    def head(h, _):
        s0 = lax.dot_general(keys_ref[h, 0], q_ref[2 * h], _NT, precision=lax.Precision.HIGHEST,
                             preferred_element_type=F32)
        s1 = lax.dot_general(keys_ref[h, 1], q_ref[2 * h + 1], _NT, precision=lax.Precision.HIGHEST,
                             preferred_element_type=F32)
        v0 = _top_sorted(s0, K)
        v1 = jnp.concatenate(_top_sorted(s1, K), axis=0)
        hk = K // 2
        cands = ([v0[0] + v1[:hk], v0[0] + v1[hk:]]
                 + [v0[a] + v1[:hk] for a in range(1, hk)]
                 + [jnp.concatenate(v0[hk:], axis=0) + v1[0:1]])
        sums = cands
        z = None
        for r in range(K):
            m = functools.reduce(jnp.maximum, cands)
            m = jnp.max(m, axis=0, keepdims=True)
            if r == 0:
                top = m
                z = jnp.ones_like(m)
            else:
                z = z + jnp.exp(m - top)
            if r + 1 < K:
                cands = [jnp.where(c == m, -jnp.inf, c) for c in cands]
        c = jnp.exp(s0 - v0[0]) / z
        inf = jnp.inf

        def smallest(sm, vals):
            return jnp.min(jnp.where(sm >= m, vals, inf), axis=0, keepdims=True)

        thetas = [jnp.minimum(smallest(sums[0], v1[:hk]), smallest(sums[1], v1[hk:]))]
        thetas += [smallest(sums[a + 1], v1[:hk]) for a in range(1, hk)]
        theta_hi = jnp.where(sums[hk + 1] >= m, v1[0:1], inf)
        thetas += [theta_hi[a:a + 1] for a in range(K - hk)]
        theta = jnp.full_like(s0, inf)
        for a in range(K):
            theta = jnp.where(s0 == v0[a], thetas[a], theta)
        n_rows = th_ref.shape[2]
        for n in range(th_ref.shape[1]):
            th_ref[h, n] = theta[n * n_rows:(n + 1) * n_rows]
            c_ref[h, n] = c[n * n_rows:(n + 1) * n_rows]
        s1_ref[h] = s1
        e1_ref[h] = jnp.exp(s1 - v1[0:1])
        return 0

    lax.fori_loop(0, PEER_HEADS, head, 0)


def peer_route(q_slots, sub_keys, *, S, tq, n_rows):
    arr = jax.ShapeDtypeStruct((PEER_HEADS, PEER_N_KEYS, S), F32)
    spec = pl.BlockSpec((PEER_HEADS, PEER_N_KEYS, tq), lambda i: (0, 0, i))
    n_tiles = PEER_N_KEYS // n_rows
    arr_t = jax.ShapeDtypeStruct((PEER_HEADS, n_tiles, n_rows, S), F32)
    spec_t = pl.BlockSpec((PEER_HEADS, n_tiles, n_rows, tq), lambda i: (0, 0, 0, i))
    return pl.pallas_call(
        _peer_route_kernel,
        grid=(S // tq,),
        in_specs=[pl.BlockSpec((2 * PEER_HEADS, tq, LANE), lambda i: (0, i, 0)),
                  pl.BlockSpec(sub_keys.shape, lambda i: (0, 0, 0, 0))],
        out_specs=[spec_t, spec, spec, spec_t],
        out_shape=[arr_t, arr, arr, arr_t],
        compiler_params=_cparams("parallel"),
        name="peer_route",
    )(q_slots, sub_keys)


def _peer_expert_kernel(x_ref, u_ref, v_ref, tha_ref, thb_ref, ca_ref, cb_ref, s1_ref, e1_ref,
                        y_ref, ha_scr, hb_scr, wa_scr, wb_scr):
    k = pl.program_id(1)
    te, tq = ha_scr.shape
    n_rows = te // PEER_N_KEYS
    D = y_ref.shape[1]
    nc = D // n_rows

    @pl.when(k == 0)
    def _():
        y_ref[...] = jnp.zeros_like(y_ref)
        hb_scr[...] = jnp.zeros_like(hb_scr)
        wa_scr[...] = jnp.zeros_like(wa_scr)

    def phase(u_lo, h_out, w_prev, h_prev, w_out, th_ref, c_ref):
        rb = 32

        def y_chunk(r):
            cols = pl.ds(r * nc, nc)
            y_ref[:, cols] += lax.dot_general(w_prev[...], v_ref[pl.ds(u_lo, te), cols], _TN,
                                              preferred_element_type=F32)

        def h_half(n):
            toks = pl.ds(n * (tq // 2), tq // 2)
            h_out[:, toks] = lax.dot_general(u_ref[pl.ds(u_lo, te), :], x_ref[toks, :], _NT,
                                             preferred_element_type=F32)

        def gate_block(tc, sb):
            tl = pl.ds(tc * LANE, LANE)
            keys = pl.ds(sb * rb, rb)
            gs = [None] * n_rows
            for h in range(PEER_HEADS):
                s1b = s1_ref[h, keys, tl]
                e1b = e1_ref[h, keys, tl]
                for r in range(n_rows):
                    t = jnp.where(s1b >= th_ref[h, pl.ds(r, 1), tl], e1b, 0.0) * c_ref[h, pl.ds(r, 1), tl]
                    gs[r] = t if gs[r] is None else gs[r] + t
            for r in range(n_rows):
                rows = pl.ds(r * PEER_N_KEYS + sb * rb, rb)
                hp = h_prev[rows, tl]
                act = 0.5 * hp * (1.0 + lax.erf(hp * math.sqrt(0.5)))
                w_out[rows, tl] = (gs[r] * act).astype(w_out.dtype)

        blocks = [(tc, sb) for tc in range(tq // LANE) for sb in range(PEER_N_KEYS // rb)]
        pieces = [functools.partial(y_chunk, r) for r in range(n_rows)] + [functools.partial(h_half, n) for n in range(2)]
        weights = [1] * n_rows + [2, 2]
        per = len(blocks) // sum(weights)
        b0 = 0
        for piece, wgt in zip(pieces, weights):
            piece()
            for tc, sb in blocks[b0:b0 + per * wgt]:
                gate_block(tc, sb)
            b0 += per * wgt
        for tc, sb in blocks[b0:]:
            gate_block(tc, sb)

    phase(0, ha_scr, wa_scr, hb_scr, wb_scr, tha_ref, ca_ref)
    phase(te, hb_scr, wb_scr, ha_scr, wa_scr, thb_ref, cb_ref)


def peer_experts(x_bf, u_bf, v_bf, th_t, s1, e1, c_t, *, tq, te):
    S, D = x_bf.shape
    E = u_bf.shape[0]
    n_tiles = E // te
    n_blocks = n_tiles // 2
    n_rows = te // PEER_N_KEYS
    assert th_t.shape == c_t.shape == (PEER_HEADS, n_tiles, n_rows, S)
    rt = pl.BlockSpec((PEER_HEADS, PEER_N_KEYS, tq), lambda i, k: (0, 0, i))
    row_a = pl.BlockSpec((PEER_HEADS, None, n_rows, tq), lambda i, k: (0, jnp.maximum(2 * k - 1, 0), 0, i))
    row_b = pl.BlockSpec((PEER_HEADS, None, n_rows, tq),
                         lambda i, k: (0, jnp.minimum(2 * k, n_tiles - 1), 0, i))
    return pl.pallas_call(
        _peer_expert_kernel,
        grid=(S // tq, n_blocks + 1),
        in_specs=[pl.BlockSpec((tq, D), lambda i, k: (i, 0)),
                  pl.BlockSpec((2 * te, D), lambda i, k: (jnp.minimum(k, n_blocks - 1), 0)),
                  pl.BlockSpec((2 * te, D), lambda i, k: (jnp.maximum(k - 1, 0), 0)),
                  row_a, row_b, row_a, row_b, rt, rt],
        out_specs=pl.BlockSpec((tq, D), lambda i, k: (i, 0)),
        out_shape=jax.ShapeDtypeStruct((S, D), F32),
        scratch_shapes=[pltpu.VMEM((te, tq), F32), pltpu.VMEM((te, tq), F32),
                        pltpu.VMEM((te, tq), BF16), pltpu.VMEM((te, tq), BF16)],
        compiler_params=_cparams("parallel", "arbitrary"),
        name="peer_experts",
    )(x_bf, u_bf, v_bf, th_t, th_t, c_t, c_t, s1, e1)


def _pad_cols(w, width):
    return jnp.pad(w, ((0, 0), (0, width - w.shape[1])))


def _rot_cols(w):
    h = w.shape[1] // 2
    return jnp.concatenate([-w[:, h:], w[:, :h]], axis=1)


def _split_w_in(w):
    widths = (512, 512, 512, 512, 128, 128, 512, 512, 512, D_Q_LORA, D_KV_LORA, D_ROPE)
    out, o = [], 0
    for n in widths:
        out.append(w[:, o:o + n])
        o += n
    return out


def _layer_weights(w_in, a_scale, b_scale, c_scale):
    (a_q, a_k, a_v, b_q, b_k, b_v, c_q, c_k, c_v, d_cq, d_ckv, d_kr) = _split_w_in(w_in)
    b_k0, b_k1 = b_k[:, :64], b_k[:, 64:]
    b_v0, b_v1 = b_v[:, :64], b_v[:, 64:]
    w_abc = jnp.concatenate(
        [a_q, a_k, a_v,
         b_q, _pad_cols(b_k0, LANE), _pad_cols(b_k1, LANE), _pad_cols(b_v0, LANE), _pad_cols(b_v1, LANE),
         c_q, c_k, c_v], axis=1).astype(BF16)
    ones = lambda n: jnp.ones((n,), F32)
    s_abc = jnp.concatenate([ones(512) * a_scale, ones(1024),
                             ones(512) * b_scale, ones(512),
                             ones(512) * c_scale, ones(1024)])[None, :]
    w_d = jnp.concatenate([d_cq, d_ckv, _pad_cols(d_kr, LANE), _pad_cols(_rot_cols(d_kr), LANE)],
                          axis=1).astype(BF16)
    return w_abc, s_abc, w_d


def _mla_weights(w_uq, w_ukv):
    wq = w_uq.reshape(D_Q_LORA, D_HEADS, D_NOPE + D_ROPE)
    nope, pe = wq[..., :D_NOPE], wq[..., D_NOPE:]
    pe_rot = jnp.concatenate([-pe[..., D_ROPE // 2:], pe[..., :D_ROPE // 2]], axis=-1)
    z64 = jnp.zeros((D_Q_LORA, D_HEADS, LANE - D_ROPE), F32)
    z128 = jnp.zeros((D_Q_LORA, D_HEADS, D_NOPE), F32)
    wqa = jnp.concatenate([nope, pe, z64], axis=-1).reshape(D_Q_LORA, -1).astype(BF16)
    wqb = jnp.concatenate([z128, pe_rot, z64], axis=-1).reshape(D_Q_LORA, -1).astype(BF16)
    wkv = w_ukv.reshape(D_KV_LORA, D_HEADS, D_NOPE + D_V)
    wkv = jnp.concatenate([wkv[..., :D_NOPE].reshape(D_KV_LORA, -1),
                           wkv[..., D_NOPE:].reshape(D_KV_LORA, -1)], axis=1).astype(BF16)
    return wqa, wqb, wkv


def _rope_tables(positions):
    half = D_ROPE // 2
    inv = ROPE_THETA ** (-jnp.arange(half, dtype=F32) / half)
    ang = positions.astype(F32)[..., None] * inv
    cos, sin = jnp.cos(ang), jnp.sin(ang)
    S = positions.shape[0]
    one = jnp.ones((S, D_NOPE), F32)
    z = jnp.zeros((S, LANE - D_ROPE), F32)
    cosm = jnp.concatenate([one, cos, cos, z], axis=1)
    sinm = jnp.concatenate([jnp.zeros((S, D_NOPE), F32), sin, sin, z], axis=1)
    return cosm, sinm


def kernel(x, positions, t5_table, w_in, w_gate, b_gate, w_branch, w_out, swa_sinks, diff_lambda_q1, diff_lambda_k1, diff_lambda_q2, diff_lambda_k2, diff_subln_g, mla_q_norm_g, mla_kv_norm_g, mla_w_uq, mla_w_ukv, ln1_g, ln1_b, peer_w_q, peer_sub_keys, peer_u, peer_v, ln2_g, ln2_b):
    B, S, D = x.shape
    assert B == 1 and D == D_MODEL and S % MOBA_BLOCK == 0
    depth = w_in.shape[0]
    alpha = (2.0 * depth) ** 0.25
    tm = min(512, S)
    T = min(512, S)

    tab_a = t5_table[:, :A_HEADS]
    tab_b = t5_table[:, A_HEADS:A_HEADS + B_HEADS]
    tab_c = t5_table[:, A_HEADS + B_HEADS:]
    bias_a = t5_tiles(tab_a, T=T, n_near=_n_near(T) + 1)[:, None]
    bias_c = t5_tiles(tab_c, T=T, n_near=_n_near(T) + 1)[:, None]
    band_b = t5_band(tab_b, tq=tm, window=SWA_WINDOW)
    cosm, sinm = _rope_tables(positions[0])

    xf = x[0]
    xb = xf.astype(BF16)
    for layer in range(depth):
        lam_init = 0.8 - 0.6 * math.exp(-0.3 * layer)
        w_abc, s_abc, w_d = _layer_weights(w_in[layer], 128 ** -0.5 * LOG2E, B_HEAD_DIM ** -0.5 * LOG2E,
                                           C_HEAD_DIM ** -0.5 * LOG2E)
        hs = proj_slots(xb, w_abc, s_abc, BF16, tm=tm, slots_per_step=8)
        hd = proj_slots(xb, w_d, jnp.ones((1, w_d.shape[1]), F32), F32, tm=tm, slots_per_step=8)

        nb = S // MOBA_BLOCK
        nbp = -(-nb // LANE) * LANE
        kmean = jnp.pad(moba_kmean(hs, 4, S), ((0, 0), (0, nbp - nb), (0, 0)))
        selb = moba_select(hs, 0, kmean, S, tq=min(1024, S))
        o_a = flash_pipe("A", hs, hs, hs, S=S, T=T, q_slot=lambda u: u, k_slot=lambda u: 4 + u,
                    v_slot=lambda u: 8 + u, bias=bias_a, selb=selb)
        sinks = ((swa_sinks[layer] - tab_b[T5_BUCKETS - 1]) * LOG2E).reshape(B_HEADS // 2, 2)
        o_b = swa_attention(hs, band_b, sinks, S=S, tq=tm, q_slot0=12, k_slot0=16, v_slot0=18)
        lam_params = jnp.stack([diff_lambda_q1[layer], diff_lambda_k1[layer],
                                diff_lambda_q2[layer], diff_lambda_k2[layer]])
        o_c = flash_pipe("C", hs, hs, hs, S=S, T=T, q_slot=lambda u: 20 + u, k_slot=lambda u: 24 + u,
                    v_slot=lambda u: 28 + u, bias=bias_c, lam_params=lam_params,
                    subln_g=diff_subln_g[layer][None, :], lam_init=lam_init)
        wqa, wqb, wkv = _mla_weights(mla_w_uq[layer], mla_w_ukv[layer])
        q_d, k_d, v_d = mla_prep(hd, mla_q_norm_g[layer][None, :], mla_kv_norm_g[layer][None, :],
                                 wqa, wqb, wkv, cosm, sinm, S=S, tm=tm)
        o_d = flash_pipe("D", q_d, k_d, v_d, S=S, T=T, q_slot=lambda u: u, k_slot=lambda u: u,
                    v_slot=lambda u: u)

        merged = merge(xb, (o_a, o_b, o_c, o_d),
                       w_gate[layer].astype(BF16), b_gate[layer][:, None, :], w_branch[layer].astype(BF16), tm=tm, tn=512)
        xf, xb = out_ln(xf, merged, w_out[layer].astype(BF16), ln1_g[layer][None, :],
                        ln1_b[layer][None, :], alpha=alpha, tm=tm)

        q_slots = proj_slots(xb, peer_w_q[layer].astype(BF16), jnp.ones((1, D), F32), F32,
                             tm=tm, slots_per_step=8)
        te = 512
        th_t, s1, e1, c_t = peer_route(q_slots, peer_sub_keys[layer], S=S, tq=tm, n_rows=te // PEER_N_KEYS)
        y = peer_experts(xb, peer_u[layer].astype(BF16), peer_v[layer].astype(BF16),
                         th_t, s1, e1, c_t, tq=tm, te=te)
        xf, xb = add_ln(xf, y, ln2_g[layer][None, :], ln2_b[layer][None, :], alpha=alpha, tm=tm)
    return xf[None]
```

```python
import functools
import math

import jax
import jax.numpy as jnp
from jax import lax
from jax.experimental import pallas as pl
from jax.experimental.pallas import tpu as pltpu

F32 = jnp.float32
BF16 = jnp.bfloat16
LANE = 128
NEG = -1e30
LOG2E = math.log2(math.e)

D_MODEL = 2048
A_HEADS = 4
MOBA_BLOCK = 256
MOBA_TOPK = 3
B_HEADS = 8
B_KV_HEADS = 2
B_HEAD_DIM = 64
SWA_WINDOW = 128
C_HEADS = 4
C_HEAD_DIM = 64
D_HEADS = 4
D_Q_LORA = 512
D_KV_LORA = 256
D_NOPE = 128
D_ROPE = 64
D_V = 128
ROPE_THETA = 10000.0
N_BRANCHES = 4
BRANCH_W = 512
T5_BUCKETS = 32
T5_MAX_EXACT = T5_BUCKETS // 2
T5_MAX_DISTANCE = 1024
PEER_HEADS = 8
PEER_N_KEYS = 128
PEER_D_KEY = 128
PEER_TOPK = 16
LN_EPS = 1e-5
RMS_EPS = 1e-6

VMEM_LIMIT_MB = 56


def _cparams(*sem):
    return pltpu.CompilerParams(dimension_semantics=sem,
                                vmem_limit_bytes=VMEM_LIMIT_MB * 1024 * 1024)


_NT = (((1,), (1,)), ((), ()))
_TN = (((0,), (0,)), ((), ()))


def _proj_kernel(x_ref, w_ref, s_ref, o_ref):
    acc = jnp.dot(x_ref[...], w_ref[...], preferred_element_type=F32) * s_ref[...]
    for c in range(o_ref.shape[0]):
        o_ref[c] = acc[:, c * LANE:(c + 1) * LANE].astype(o_ref.dtype)


def proj_slots(x_bf, w_bf, colscale, out_dtype, *, tm, slots_per_step):
    S, K = x_bf.shape
    N = w_bf.shape[1]
    tn = slots_per_step * LANE
    return pl.pallas_call(
        _proj_kernel,
        grid=(N // tn, S // tm),
        in_specs=[pl.BlockSpec((tm, K), lambda j, i: (i, 0)),
                  pl.BlockSpec((K, tn), lambda j, i: (0, j)),
                  pl.BlockSpec((1, tn), lambda j, i: (0, j))],
        out_specs=pl.BlockSpec((slots_per_step, tm, LANE), lambda j, i: (j, i, 0)),
        out_shape=jax.ShapeDtypeStruct((N // LANE, S, LANE), out_dtype),
        compiler_params=_cparams("parallel", "parallel"),
        name="proj_slots",
    )(x_bf, w_bf, colscale)


def _t5_tiles_kernel(tab_ref, o_ref, *, T, n_heads):
    dj = pl.program_id(0)
    row = lax.broadcasted_iota(jnp.int32, (T, T), 0)
    col = lax.broadcasted_iota(jnp.int32, (T, T), 1)
    bucket = _t5_bucket(jnp.maximum(row - col + dj * T, 0))
    for h in range(n_heads):
        last = tab_ref[T5_BUCKETS - 1, h]
        acc = jnp.zeros((T, T), F32)
        for b in range(T5_BUCKETS - 1):
            acc = jnp.where(bucket == b, (tab_ref[b, h] - last) * LOG2E, acc)
        o_ref[h, 0] = acc


def t5_tiles(tab, *, T, n_near):
    H = tab.shape[1]
    return pl.pallas_call(
        functools.partial(_t5_tiles_kernel, T=T, n_heads=H),
        grid=(n_near,),
        in_specs=[pl.BlockSpec(memory_space=pltpu.SMEM)],
        out_specs=pl.BlockSpec((H, 1, T, T), lambda d: (0, d, 0, 0)),
        out_shape=jax.ShapeDtypeStruct((H, n_near, T, T), F32),
        compiler_params=_cparams("parallel"),
        name="t5_tiles",
    )(tab)


def _n_near(T):
    return (T5_MAX_DISTANCE - 1 + T - 1) // T + 1


def _t5_bucket(n):
    nf = jnp.maximum(n, 1).astype(F32)
    large = T5_MAX_EXACT + (jnp.log(nf / T5_MAX_EXACT) / math.log(T5_MAX_DISTANCE / T5_MAX_EXACT)
                            * (T5_BUCKETS - T5_MAX_EXACT)).astype(jnp.int32)
    return jnp.where(n < T5_MAX_EXACT, n, jnp.minimum(large, T5_BUCKETS - 1))


def _t5_band_kernel(tab_ref, o_ref, *, window):
    h = pl.program_id(0)
    rows, cols = o_ref.shape
    dist = (lax.broadcasted_iota(jnp.int32, (rows, cols), 0)
            - lax.broadcasted_iota(jnp.int32, (rows, cols), 1) + window)
    bucket = _t5_bucket(jnp.maximum(dist, 0))
    last = tab_ref[T5_BUCKETS - 1, h]
    acc = jnp.zeros((rows, cols), F32)
    for b in range(T5_BUCKETS - 1):
        acc = jnp.where(bucket == b, (tab_ref[b, h] - last) * LOG2E, acc)
    o_ref[...] = jnp.where((dist >= 0) & (dist < window), acc, NEG)


def t5_band(tab, *, tq, window):
    H = tab.shape[1]
    cols = tq + 2 * window
    return pl.pallas_call(
        functools.partial(_t5_band_kernel, window=window),
        grid=(H,),
        in_specs=[pl.BlockSpec(memory_space=pltpu.SMEM)],
        out_specs=pl.BlockSpec((None, tq, cols), lambda h: (h, 0, 0)),
        out_shape=jax.ShapeDtypeStruct((H, tq, cols), F32),
        compiler_params=_cparams("parallel"),
        name="t5_band",
    )(tab)


def _swa_kernel(q_ref, k_ref, v_ref, band_ref, sink_ref, o_ref, *, window):
    u = pl.program_id(0)
    qi = pl.program_id(1)
    tq = q_ref.shape[0]
    nk = tq + window
    half = LANE // 2
    start = pl.multiple_of(jnp.maximum(qi * tq - window, 0), window)
    kb = k_ref[pl.ds(start, nk), :]
    vb = v_ref[pl.ds(start, nk), :]
    q = q_ref[...]
    z = jnp.zeros((tq, half), q.dtype)
    outs = []
    for c in range(2):
        qc = jnp.concatenate([q[:, c * half:(c + 1) * half], z], axis=1)
        band = jnp.where(qi == 0, band_ref[c, :, window:], band_ref[c, :, :nk])
        s = lax.dot_general(qc, kb, _NT, preferred_element_type=F32) + band
        sink = sink_ref[u, c]
        m = jnp.maximum(jnp.max(s, axis=-1, keepdims=True), sink)
        p = jnp.exp2(s - m)
        l = jnp.sum(p, axis=-1, keepdims=True) + jnp.exp2(sink - m)
        o = jnp.dot(p.astype(vb.dtype), vb, preferred_element_type=F32) / l
        outs.append(o[:, :half])
    o_ref[...] = jnp.concatenate(outs, axis=1).astype(o_ref.dtype)


def swa_attention(hs, band, sinks, *, S, tq, q_slot0, k_slot0, v_slot0):
    n_units = B_HEADS // 2
    per_kv = n_units // B_KV_HEADS
    cols = band.shape[-1]
    return pl.pallas_call(
        functools.partial(_swa_kernel, window=SWA_WINDOW),
        grid=(n_units, S // tq),
        in_specs=[pl.BlockSpec((None, tq, LANE), lambda u, i: (q_slot0 + u, i, 0)),
                  pl.BlockSpec((None, S, LANE), lambda u, i: (k_slot0 + u // per_kv, 0, 0)),
                  pl.BlockSpec((None, S, LANE), lambda u, i: (v_slot0 + u // per_kv, 0, 0)),
                  pl.BlockSpec((2, tq, cols), lambda u, i: (u, 0, 0)),
                  pl.BlockSpec(memory_space=pltpu.SMEM)],
        out_specs=pl.BlockSpec((None, tq, LANE), lambda u, i: (u, i, 0)),
        out_shape=jax.ShapeDtypeStruct((n_units, S, LANE), BF16),
        compiler_params=_cparams("parallel", "arbitrary"),
        name="swa",
    )(hs, hs, hs, band, sinks)


def _kmean_kernel(k_ref, o_ref, *, blk):
    k = k_ref[...].astype(F32)
    nb = k.shape[0] // blk
    o_ref[...] = k.reshape(nb, blk, LANE).sum(axis=1) * (1.0 / blk)


def moba_kmean(hs, k_slot0, S):
    nb = S // MOBA_BLOCK
    return pl.pallas_call(
        functools.partial(_kmean_kernel, blk=MOBA_BLOCK),
        grid=(A_HEADS,),
        in_specs=[pl.BlockSpec((None, S, LANE), lambda h: (k_slot0 + h, 0, 0))],
        out_specs=pl.BlockSpec((None, nb, LANE), lambda h: (h, 0, 0)),
        out_shape=jax.ShapeDtypeStruct((A_HEADS, nb, LANE), F32),
        compiler_params=_cparams("parallel"),
        name="moba_kmean",
    )(hs)


def _moba_select_kernel(q_ref, km_ref, o_ref, *, topk):
    tq = q_ref.shape[0]
    q = q_ref[...].astype(F32)
    gate = lax.dot_general(q, km_ref[...], _NT, precision=lax.Precision.HIGHEST,
                           preferred_element_type=F32)
    lane = lax.broadcasted_iota(jnp.int32, gate.shape, 1)
    row = lax.broadcasted_iota(jnp.int32, gate.shape, 0)
    own = pl.program_id(1) * (tq // MOBA_BLOCK) + row // MOBA_BLOCK
    nbp = gate.shape[1]
    gate = jnp.where(lane < own, gate, -jnp.inf)
    mask = jnp.where(lane == own, 0.0, NEG)
    for _ in range(topk):
        mx = jnp.max(gate, axis=-1, keepdims=True)
        idx = jnp.min(jnp.where(gate == mx, lane, nbp), axis=-1, keepdims=True)
        hit = lane == idx
        mask = jnp.where(hit, jnp.where(mx > -jnp.inf, 0.0, mask), mask)
        gate = jnp.where(hit, -jnp.inf, gate)
    o_ref[...] = mask.astype(o_ref.dtype)


def moba_select(hs, q_slot0, kmean_pad, S, *, tq):
    nbp = kmean_pad.shape[1]
    return pl.pallas_call(
        functools.partial(_moba_select_kernel, topk=MOBA_TOPK),
        grid=(A_HEADS, S // tq),
        in_specs=[pl.BlockSpec((None, tq, LANE), lambda h, i: (q_slot0 + h, i, 0)),
                  pl.BlockSpec((None, nbp, LANE), lambda h, i: (h, 0, 0))],
        out_specs=pl.BlockSpec((None, tq, nbp), lambda h, i: (h, i, 0)),
        out_shape=jax.ShapeDtypeStruct((A_HEADS, S, nbp), BF16),
        compiler_params=_cparams("parallel", "parallel"),
        name="moba_select",
    )(hs, kmean_pad)


def _flash_pipe_kernel(*refs, mode, T, n_near, lam_init):
    it = iter(refs)
    q_ref, k_ref, v_ref = next(it), next(it), next(it)
    bias_ref = next(it) if mode in "AC" else None
    selb_ref = next(it) if mode == "A" else None
    lam_ref, g_ref = (next(it), next(it)) if mode == "C" else (None, None)
    o_ref = next(it)
    m_scr, l_scr, acc_scr = next(it), next(it), next(it)
    rest = list(it)
    s_scr, p_scr = rest[:len(rest) // 2], rest[len(rest) // 2:]

    qi = pl.program_id(1)
    q = q_ref[...]
    half = LANE // 2
    if mode == "A":
        qs = [jnp.concatenate([q, selb_ref[...]], axis=1)]
    elif mode == "C":
        lane = lax.broadcasted_iota(jnp.int32, q.shape, 1)
        zq = jnp.zeros_like(q)
        qs = [jnp.where(lane < half, q, zq), jnp.where(lane >= half, q, zq)]
    else:
        qs = [q]
    n_sub = len(qs)

    def k_tile(j):
        kt = k_ref[pl.ds(pl.multiple_of(j * T, T), T), :]
        if mode == "A":
            n_blk = selb_ref.shape[-1]
            blk = lax.broadcasted_iota(jnp.int32, (T, n_blk), 1)
            row = lax.broadcasted_iota(jnp.int32, (T, n_blk), 0)
            own = j * (T // MOBA_BLOCK) + row // MOBA_BLOCK
            kt = jnp.concatenate([kt, jnp.where(blk == own, 1.0, 0.0).astype(kt.dtype)], axis=1)
        return kt

    def v_tile(j):
        return v_ref[pl.ds(pl.multiple_of(j * T, T), T), :]

    def scores(c, kt):
        return lax.dot_general(qs[c], kt, _NT, preferred_element_type=F32)

    kt0 = k_tile(0)
    for c in range(n_sub):
        s_scr[c][0] = scores(c, kt0)
        p_scr[c][1] = jnp.zeros((T, T), p_scr[c].dtype)
        m_scr[c] = jnp.full((T, LANE), NEG, F32)
        l_scr[c] = jnp.zeros((T, LANE), F32)
        acc_scr[c] = jnp.zeros(acc_scr.shape[1:], F32)

    def step(j, slot, near):
        nslot = 1 - slot
        kt_next = k_tile(jnp.minimum(j + 1, qi))
        vt_prev = v_tile(jnp.maximum(j - 1, 0))
        if near:
            dj = qi - j
            row = lax.broadcasted_iota(jnp.int32, (T, T), 0)
            col = lax.broadcasted_iota(jnp.int32, (T, T), 1)
            valid = row - col + dj * T >= 0
        for c in range(n_sub):
            pv = jnp.dot(p_scr[c][nslot], vt_prev, preferred_element_type=F32)
            s = s_scr[c][slot]
            s_scr[c][nslot] = scores(c, kt_next)
            if near:
                if bias_ref is not None:
                    s = s + bias_ref[0, dj]
                s = jnp.where(valid, s, NEG)
            m_prev = m_scr[c]
            m_new = jnp.maximum(m_prev, jnp.max(s, axis=-1, keepdims=True))
            alpha = jnp.exp2(m_prev - m_new)
            ps = [jnp.exp2(s[:, i * LANE:(i + 1) * LANE] - m_new) for i in range(T // LANE)]
            p_scr[c][slot] = jnp.concatenate(ps, axis=1).astype(p_scr[c].dtype)
            l_scr[c] = alpha * l_scr[c] + functools.reduce(lambda a, b: a + b, ps)
            acc_scr[c] = alpha * (acc_scr[c] + pv)
            m_scr[c] = m_new

    n_pairs = jnp.maximum(qi - n_near + 1, 0) // 2

    def far_pair(j0):
        step(j0, 0, False)
        step(j0 + 1, 1, False)

    def far_quad(i, _):
        far_pair(4 * i)
        far_pair(4 * i + 2)
        return 0

    n_quads = n_pairs // 2
    lax.fori_loop(0, n_quads, far_quad, 0)

    @pl.when(n_pairs % 2 == 1)
    def _():
        far_pair(4 * n_quads)
    for t in range(n_near + 1):
        jn = 2 * n_pairs + t

        @pl.when(jn <= qi)
        def _():
            step(jn, t % 2, True)

    last = lax.rem(qi, 2)
    vt = v_tile(qi)
    outs = []
    for c in range(n_sub):
        acc = acc_scr[c] + jnp.dot(p_scr[c][last], vt, preferred_element_type=F32)
        outs.append(acc / jnp.sum(l_scr[c], axis=-1, keepdims=True))
    if mode == "C":
        lp = lam_ref[...]
        lam = (jnp.exp(jnp.sum(lp[0:1] * lp[1:2], axis=-1, keepdims=True))
               - jnp.exp(jnp.sum(lp[2:3] * lp[3:4], axis=-1, keepdims=True)) + lam_init)
        o = outs[0] - lam * outs[1]
        o = o * lax.rsqrt(jnp.mean(o * o, axis=-1, keepdims=True) + RMS_EPS) * g_ref[...]
        o = o * (1.0 - lam_init)
    else:
        o = outs[0]
    o_ref[...] = o.astype(o_ref.dtype)


def flash_pipe(mode, q_arr, k_arr, v_arr, *, S, T, q_slot, k_slot, v_slot, n_units=4, bias=None,
               selb=None, lam_params=None, subln_g=None, lam_init=0.0):
    dq = q_arr.shape[-1]
    dk = k_arr.shape[-1]
    dv = v_arr.shape[-1]
    n_near = 1 if mode == "D" else _n_near(T)
    n_sub = 2 if mode == "C" else 1
    in_specs = [pl.BlockSpec((None, T, dq), lambda u, i: (q_slot(u), i, 0)),
                pl.BlockSpec((None, S, dk), lambda u, i: (k_slot(u), 0, 0)),
                pl.BlockSpec((None, S, dv), lambda u, i: (v_slot(u), 0, 0))]
    args = [q_arr, k_arr, v_arr]
    if bias is not None:
        in_specs.append(pl.BlockSpec((None,) + bias.shape[1:], lambda u, i: (u, 0, 0, 0, 0)))
        args.append(bias)
    if selb is not None:
        in_specs.append(pl.BlockSpec((None, T, selb.shape[-1]), lambda u, i: (u, i, 0)))
        args.append(selb)
    if lam_params is not None:
        in_specs.append(pl.BlockSpec(lam_params.shape, lambda u, i: (0, 0)))
        in_specs.append(pl.BlockSpec(subln_g.shape, lambda u, i: (0, 0)))
        args += [lam_params, subln_g]
    return pl.pallas_call(
        functools.partial(_flash_pipe_kernel, mode=mode, T=T, n_near=n_near, lam_init=lam_init),
        grid=(n_units, S // T),
        in_specs=in_specs,
        out_specs=pl.BlockSpec((None, T, LANE), lambda u, i: (u, i, 0)),
        out_shape=jax.ShapeDtypeStruct((n_units, S, LANE), BF16),
        scratch_shapes=([pltpu.VMEM((n_sub, T, LANE), F32), pltpu.VMEM((n_sub, T, LANE), F32),
                         pltpu.VMEM((n_sub, T, dv), F32)]
                        + [pltpu.VMEM((2, T, T), F32)] * n_sub + [pltpu.VMEM((2, T, T), BF16)] * n_sub),
        compiler_params=_cparams("parallel", "arbitrary"),
        name="flashp_" + mode,
    )(*args)


def _rms(x, g):
    return x * lax.rsqrt(jnp.mean(x * x, axis=-1, keepdims=True) + RMS_EPS) * g


def _mla_prep_kernel(hd_ref, gq_ref, gkv_ref, wqa_ref, wqb_ref, wkv_ref, cos_ref, sin_ref,
                     qo_ref, ko_ref, vo_ref, *, scale):
    cosm = cos_ref[...]
    sinm = sin_ref[...]
    cq = jnp.concatenate([hd_ref[c] for c in range(4)], axis=1)
    cqn = _rms(cq, gq_ref[...]).astype(BF16)
    qa = jnp.dot(cqn, wqa_ref[...], preferred_element_type=F32)
    qb = jnp.dot(cqn, wqb_ref[...], preferred_element_type=F32)
    w = 2 * LANE
    for h in range(D_HEADS):
        qe = (qa[:, h * w:(h + 1) * w] * cosm + qb[:, h * w:(h + 1) * w] * sinm) * scale
        qo_ref[h] = qe.astype(qo_ref.dtype)
    ckv = jnp.concatenate([hd_ref[4], hd_ref[5]], axis=1)
    ckvn = _rms(ckv, gkv_ref[...]).astype(BF16)
    kv = jnp.dot(ckvn, wkv_ref[...], preferred_element_type=F32)
    kpe = hd_ref[6] * cosm[:, LANE:] + hd_ref[7] * sinm[:, LANE:]
    for h in range(D_HEADS):
        ko_ref[h] = jnp.concatenate([kv[:, h * LANE:(h + 1) * LANE], kpe], axis=1).astype(ko_ref.dtype)
        vo_ref[h] = kv[:, (D_HEADS + h) * LANE:(D_HEADS + h + 1) * LANE].astype(vo_ref.dtype)


def mla_prep(hd, gq, gkv, wqa, wqb, wkv, cosm, sinm, *, S, tm):
    w = 2 * LANE
    full = lambda a: pl.BlockSpec(a.shape, lambda i: (0,) * a.ndim)
    return pl.pallas_call(
        functools.partial(_mla_prep_kernel, scale=(D_NOPE + D_ROPE) ** -0.5 * LOG2E),
        grid=(S // tm,),
        in_specs=[pl.BlockSpec((8, tm, LANE), lambda i: (0, i, 0)),
                  full(gq), full(gkv), full(wqa), full(wqb), full(wkv),
                  pl.BlockSpec((tm, w), lambda i: (i, 0)),
                  pl.BlockSpec((tm, w), lambda i: (i, 0))],
        out_specs=[pl.BlockSpec((D_HEADS, tm, w), lambda i: (0, i, 0)),
                   pl.BlockSpec((D_HEADS, tm, w), lambda i: (0, i, 0)),
                   pl.BlockSpec((D_HEADS, tm, LANE), lambda i: (0, i, 0))],
        out_shape=[jax.ShapeDtypeStruct((D_HEADS, S, w), BF16),
                   jax.ShapeDtypeStruct((D_HEADS, S, w), BF16),
                   jax.ShapeDtypeStruct((D_HEADS, S, LANE), BF16)],
        compiler_params=_cparams("parallel"),
        name="mla_prep",
    )(hd, gq, gkv, wqa, wqb, wkv, cosm, sinm)


def _merge_kernel(x_ref, oa_ref, ob_ref, oc_ref, od_ref, wg_ref, bg_ref, wbr_ref, o_ref):
    x = x_ref[...]
    acc = None
    for n, o in enumerate((oa_ref, ob_ref, oc_ref, od_ref)):
        g = jnp.dot(x, wg_ref[n], preferred_element_type=F32) + bg_ref[n]
        ocat = jnp.concatenate([o[c] for c in range(4)], axis=1)
        b = jnp.dot(ocat, wbr_ref[n], preferred_element_type=F32)
        t = jax.nn.sigmoid(g) * b
        acc = t if acc is None else acc + t
    o_ref[...] = acc.astype(o_ref.dtype)


def merge(x_bf, outs, wg, bg, wbr, *, tm, tn):
    S, D = x_bf.shape
    o_spec = pl.BlockSpec((4, tm, LANE), lambda j, i: (0, i, 0))
    return pl.pallas_call(
        _merge_kernel,
        grid=(D // tn, S // tm),
        in_specs=[pl.BlockSpec((tm, D), lambda j, i: (i, 0)), o_spec, o_spec, o_spec, o_spec,
                  pl.BlockSpec((N_BRANCHES, D, tn), lambda j, i: (0, 0, j)),
                  pl.BlockSpec((N_BRANCHES, 1, tn), lambda j, i: (0, 0, j)),
                  pl.BlockSpec((N_BRANCHES, BRANCH_W, tn), lambda j, i: (0, 0, j))],
        out_specs=pl.BlockSpec((tm, tn), lambda j, i: (i, j)),
        out_shape=jax.ShapeDtypeStruct((S, D), BF16),
        compiler_params=_cparams("parallel", "parallel"),
        name="merge",
    )(x_bf, *outs, wg, bg, wbr)


def _layer_norm(y, g, b):
    mu = jnp.mean(y, axis=-1, keepdims=True)
    var = jnp.mean(jnp.square(y - mu), axis=-1, keepdims=True)
    return (y - mu) * lax.rsqrt(var + LN_EPS) * g + b


def _out_ln_kernel(x_ref, m_ref, w_ref, g_ref, b_ref, of_ref, ob_ref, *, alpha):
    y = alpha * x_ref[...] + jnp.dot(m_ref[...], w_ref[...], preferred_element_type=F32)
    o = _layer_norm(y, g_ref[...], b_ref[...])
    of_ref[...] = o
    ob_ref[...] = o.astype(ob_ref.dtype)


def out_ln(x, merged, w_out, g, b, *, alpha, tm):
    S, D = x.shape
    row = pl.BlockSpec((tm, D), lambda i: (i, 0))
    vec = pl.BlockSpec((1, D), lambda i: (0, 0))
    return pl.pallas_call(
        functools.partial(_out_ln_kernel, alpha=alpha),
        grid=(S // tm,),
        in_specs=[row, row, pl.BlockSpec((D, D), lambda i: (0, 0)), vec, vec],
        out_specs=[row, row],
        out_shape=[jax.ShapeDtypeStruct((S, D), F32), jax.ShapeDtypeStruct((S, D), BF16)],
        compiler_params=_cparams("parallel"),
        name="out_ln",
    )(x, merged, w_out, g, b)


def _add_ln_kernel(x_ref, y_ref, g_ref, b_ref, of_ref, ob_ref, *, alpha):
    o = _layer_norm(alpha * x_ref[...] + y_ref[...], g_ref[...], b_ref[...])
    of_ref[...] = o
    ob_ref[...] = o.astype(ob_ref.dtype)


def add_ln(x, y, g, b, *, alpha, tm):
    S, D = x.shape
    row = pl.BlockSpec((tm, D), lambda i: (i, 0))
    vec = pl.BlockSpec((1, D), lambda i: (0, 0))
    return pl.pallas_call(
        functools.partial(_add_ln_kernel, alpha=alpha),
        grid=(S // tm,),
        in_specs=[row, row, vec, vec],
        out_specs=[row, row],
        out_shape=[jax.ShapeDtypeStruct((S, D), F32), jax.ShapeDtypeStruct((S, D), BF16)],
        compiler_params=_cparams("parallel"),
        name="add_ln",
    )(x, y, g, b)


def _top_sorted(s, k):
    vals = []
    cur = s
    for r in range(k):
        m = jnp.max(cur, axis=0, keepdims=True)
        vals.append(m)
        if r + 1 < k:
            cur = jnp.where(cur == m, -jnp.inf, cur)
    return vals


def _peer_route_kernel(q_ref, keys_ref, th_ref, s1_ref, e1_ref, c_ref):
    K = PEER_TOPK

    def head(h, _):
        s0 = lax.dot_general(keys_ref[h, 0], q_ref[2 * h], _NT, precision=lax.Precision.HIGHEST,
                             preferred_element_type=F32)
        s1 = lax.dot_general(keys_ref[h, 1], q_ref[2 * h + 1], _NT, precision=lax.Precision.HIGHEST,
                             preferred_element_type=F32)
        v0 = _top_sorted(s0, K)
        v1 = jnp.concatenate(_top_sorted(s1, K), axis=0)
        hk = K // 2
        cands = ([v0[0] + v1[:hk], v0[0] + v1[hk:]]
                 + [v0[a] + v1[:hk] for a in range(1, hk)]
                 + [jnp.concatenate(v0[hk:], axis=0) + v1[0:1]])
        sums = cands
        z = None
        for r in range(K):
            m = functools.reduce(jnp.maximum, cands)
            m = jnp.max(m, axis=0, keepdims=True)
            if r == 0:
                top = m
                z = jnp.ones_like(m)
            else:
                z = z + jnp.exp(m - top)
            if r + 1 < K:
                cands = [jnp.where(c == m, -jnp.inf, c) for c in cands]
        c = jnp.exp(s0 - v0[0]) / z
        inf = jnp.inf

        def smallest(sm, vals):
            return jnp.min(jnp.where(sm >= m, vals, inf), axis=0, keepdims=True)

        thetas = [jnp.minimum(smallest(sums[0], v1[:hk]), smallest(sums[1], v1[hk:]))]
        thetas += [smallest(sums[a + 1], v1[:hk]) for a in range(1, hk)]
        theta_hi = jnp.where(sums[hk + 1] >= m, v1[0:1], inf)
        thetas += [theta_hi[a:a + 1] for a in range(K - hk)]
        theta = jnp.full_like(s0, inf)
        for a in range(K):
            theta = jnp.where(s0 == v0[a], thetas[a], theta)
        n_rows = th_ref.shape[2]
        for n in range(th_ref.shape[1]):
            th_ref[h, n] = theta[n * n_rows:(n + 1) * n_rows]
            c_ref[h, n] = c[n * n_rows:(n + 1) * n_rows]
        s1_ref[h] = s1
        e1_ref[h] = jnp.exp(s1 - v1[0:1])
        return 0

    lax.fori_loop(0, PEER_HEADS, head, 0, unroll=4)


def peer_route(q_slots, sub_keys, *, S, tq, n_rows):
    arr = jax.ShapeDtypeStruct((PEER_HEADS, PEER_N_KEYS, S), F32)
    spec = pl.BlockSpec((PEER_HEADS, PEER_N_KEYS, tq), lambda i: (0, 0, i))
    n_tiles = PEER_N_KEYS // n_rows
    arr_t = jax.ShapeDtypeStruct((PEER_HEADS, n_tiles, n_rows, S), F32)
    spec_t = pl.BlockSpec((PEER_HEADS, n_tiles, n_rows, tq), lambda i: (0, 0, 0, i))
    return pl.pallas_call(
        _peer_route_kernel,
        grid=(S // tq,),
        in_specs=[pl.BlockSpec((2 * PEER_HEADS, tq, LANE), lambda i: (0, i, 0)),
                  pl.BlockSpec(sub_keys.shape, lambda i: (0, 0, 0, 0))],
        out_specs=[spec_t, spec, spec, spec_t],
        out_shape=[arr_t, arr, arr, arr_t],
        compiler_params=_cparams("parallel"),
        name="peer_route",
    )(q_slots, sub_keys)


def _peer_expert_kernel(x_ref, u_ref, v_ref, tha_ref, thb_ref, ca_ref, cb_ref, s1_ref, e1_ref,
                        y_ref, ha_scr, hb_scr, wa_scr, wb_scr):
    k = pl.program_id(1)
    te, tq = ha_scr.shape
    n_rows = te // PEER_N_KEYS
    D = y_ref.shape[1]
    nc = D // n_rows

    @pl.when(k == 0)
    def _():
        y_ref[...] = jnp.zeros_like(y_ref)
        hb_scr[...] = jnp.zeros_like(hb_scr)
        wa_scr[...] = jnp.zeros_like(wa_scr)

    def phase(u_lo, h_out, w_prev, h_prev, w_out, th_ref, c_ref):
        rb = 32

        def y_chunk(r):
            cols = pl.ds(r * nc, nc)
            y_ref[:, cols] += lax.dot_general(w_prev[...], v_ref[pl.ds(u_lo, te), cols], _TN,
                                              preferred_element_type=F32)

        def h_half(n):
            toks = pl.ds(n * (tq // 2), tq // 2)
            h_out[:, toks] = lax.dot_general(u_ref[pl.ds(u_lo, te), :], x_ref[toks, :], _NT,
                                             preferred_element_type=F32)

        def gate_block(tc, sb):
            tl = pl.ds(tc * LANE, LANE)
            keys = pl.ds(sb * rb, rb)
            gs = [None] * n_rows
            for h in range(PEER_HEADS):
                s1b = s1_ref[h, keys, tl]
                e1b = e1_ref[h, keys, tl]
                for r in range(n_rows):
                    t = jnp.where(s1b >= th_ref[h, pl.ds(r, 1), tl], e1b, 0.0) * c_ref[h, pl.ds(r, 1), tl]
                    gs[r] = t if gs[r] is None else gs[r] + t
            for r in range(n_rows):
                rows = pl.ds(r * PEER_N_KEYS + sb * rb, rb)
                hp = h_prev[rows, tl]
                act = 0.5 * hp * (1.0 + lax.erf(hp * math.sqrt(0.5)))
                w_out[rows, tl] = (gs[r] * act).astype(w_out.dtype)

        blocks = [(tc, sb) for tc in range(tq // LANE) for sb in range(PEER_N_KEYS // rb)]
        pieces = [functools.partial(y_chunk, r) for r in range(n_rows)] + [functools.partial(h_half, n) for n in range(2)]
        weights = [1] * n_rows + [2, 2]
        per = len(blocks) // sum(weights)
        b0 = 0
        for piece, wgt in zip(pieces, weights):
            piece()
            for tc, sb in blocks[b0:b0 + per * wgt]:
                gate_block(tc, sb)
            b0 += per * wgt
        for tc, sb in blocks[b0:]:
            gate_block(tc, sb)

    phase(0, ha_scr, wa_scr, hb_scr, wb_scr, tha_ref, ca_ref)
    phase(te, hb_scr, wb_scr, ha_scr, wa_scr, thb_ref, cb_ref)


def peer_experts(x_bf, u_bf, v_bf, th_t, s1, e1, c_t, *, tq, te):
    S, D = x_bf.shape
    E = u_bf.shape[0]
    n_tiles = E // te
    n_blocks = n_tiles // 2
    n_rows = te // PEER_N_KEYS
    assert th_t.shape == c_t.shape == (PEER_HEADS, n_tiles, n_rows, S)
    rt = pl.BlockSpec((PEER_HEADS, PEER_N_KEYS, tq), lambda i, k: (0, 0, i))
    row_a = pl.BlockSpec((PEER_HEADS, None, n_rows, tq), lambda i, k: (0, jnp.maximum(2 * k - 1, 0), 0, i))
    row_b = pl.BlockSpec((PEER_HEADS, None, n_rows, tq),
                         lambda i, k: (0, jnp.minimum(2 * k, n_tiles - 1), 0, i))
    return pl.pallas_call(
        _peer_expert_kernel,
        grid=(S // tq, n_blocks + 1),
        in_specs=[pl.BlockSpec((tq, D), lambda i, k: (i, 0)),
                  pl.BlockSpec((2 * te, D), lambda i, k: (jnp.minimum(k, n_blocks - 1), 0)),
                  pl.BlockSpec((2 * te, D), lambda i, k: (jnp.maximum(k - 1, 0), 0)),
                  row_a, row_b, row_a, row_b, rt, rt],
        out_specs=pl.BlockSpec((tq, D), lambda i, k: (i, 0)),
        out_shape=jax.ShapeDtypeStruct((S, D), F32),
        scratch_shapes=[pltpu.VMEM((te, tq), F32), pltpu.VMEM((te, tq), F32),
                        pltpu.VMEM((te, tq), BF16), pltpu.VMEM((te, tq), BF16)],
        compiler_params=_cparams("parallel", "arbitrary"),
        name="peer_experts",
    )(x_bf, u_bf, v_bf, th_t, th_t, c_t, c_t, s1, e1)


def _pad_cols(w, width):
    return jnp.pad(w, ((0, 0), (0, width - w.shape[1])))


def _rot_cols(w):
    h = w.shape[1] // 2
    return jnp.concatenate([-w[:, h:], w[:, :h]], axis=1)


def _split_w_in(w):
    widths = (512, 512, 512, 512, 128, 128, 512, 512, 512, D_Q_LORA, D_KV_LORA, D_ROPE)
    out, o = [], 0
    for n in widths:
        out.append(w[:, o:o + n])
        o += n
    return out


def _layer_weights(w_in, a_scale, b_scale, c_scale):
    (a_q, a_k, a_v, b_q, b_k, b_v, c_q, c_k, c_v, d_cq, d_ckv, d_kr) = _split_w_in(w_in)
    b_k0, b_k1 = b_k[:, :64], b_k[:, 64:]
    b_v0, b_v1 = b_v[:, :64], b_v[:, 64:]
    w_abc = jnp.concatenate(
        [a_q, a_k, a_v,
         b_q, _pad_cols(b_k0, LANE), _pad_cols(b_k1, LANE), _pad_cols(b_v0, LANE), _pad_cols(b_v1, LANE),
         c_q, c_k, c_v], axis=1).astype(BF16)
    ones = lambda n: jnp.ones((n,), F32)
    s_abc = jnp.concatenate([ones(512) * a_scale, ones(1024),
                             ones(512) * b_scale, ones(512),
                             ones(512) * c_scale, ones(1024)])[None, :]
    w_d = jnp.concatenate([d_cq, d_ckv, _pad_cols(d_kr, LANE), _pad_cols(_rot_cols(d_kr), LANE)],
                          axis=1).astype(BF16)
    return w_abc, s_abc, w_d


def _mla_weights(w_uq, w_ukv):
    wq = w_uq.reshape(D_Q_LORA, D_HEADS, D_NOPE + D_ROPE)
    nope, pe = wq[..., :D_NOPE], wq[..., D_NOPE:]
    pe_rot = jnp.concatenate([-pe[..., D_ROPE // 2:], pe[..., :D_ROPE // 2]], axis=-1)
    z64 = jnp.zeros((D_Q_LORA, D_HEADS, LANE - D_ROPE), F32)
    z128 = jnp.zeros((D_Q_LORA, D_HEADS, D_NOPE), F32)
    wqa = jnp.concatenate([nope, pe, z64], axis=-1).reshape(D_Q_LORA, -1).astype(BF16)
    wqb = jnp.concatenate([z128, pe_rot, z64], axis=-1).reshape(D_Q_LORA, -1).astype(BF16)
    wkv = w_ukv.reshape(D_KV_LORA, D_HEADS, D_NOPE + D_V)
    wkv = jnp.concatenate([wkv[..., :D_NOPE].reshape(D_KV_LORA, -1),
                           wkv[..., D_NOPE:].reshape(D_KV_LORA, -1)], axis=1).astype(BF16)
    return wqa, wqb, wkv


def _rope_tables(positions):
    half = D_ROPE // 2
    inv = ROPE_THETA ** (-jnp.arange(half, dtype=F32) / half)
    ang = positions.astype(F32)[..., None] * inv
    cos, sin = jnp.cos(ang), jnp.sin(ang)
    S = positions.shape[0]
    one = jnp.ones((S, D_NOPE), F32)
    z = jnp.zeros((S, LANE - D_ROPE), F32)
    cosm = jnp.concatenate([one, cos, cos, z], axis=1)
    sinm = jnp.concatenate([jnp.zeros((S, D_NOPE), F32), sin, sin, z], axis=1)
    return cosm, sinm


def kernel(x, positions, t5_table, w_in, w_gate, b_gate, w_branch, w_out, swa_sinks, diff_lambda_q1, diff_lambda_k1, diff_lambda_q2, diff_lambda_k2, diff_subln_g, mla_q_norm_g, mla_kv_norm_g, mla_w_uq, mla_w_ukv, ln1_g, ln1_b, peer_w_q, peer_sub_keys, peer_u, peer_v, ln2_g, ln2_b):
    B, S, D = x.shape
    assert B == 1 and D == D_MODEL and S % MOBA_BLOCK == 0
    depth = w_in.shape[0]
    alpha = (2.0 * depth) ** 0.25
    tm = min(512, S)
    T = min(512, S)

    tab_a = t5_table[:, :A_HEADS]
    tab_b = t5_table[:, A_HEADS:A_HEADS + B_HEADS]
    tab_c = t5_table[:, A_HEADS + B_HEADS:]
    bias_a = t5_tiles(tab_a, T=T, n_near=_n_near(T) + 1)[:, None]
    bias_c = t5_tiles(tab_c, T=T, n_near=_n_near(T) + 1)[:, None]
    band_b = t5_band(tab_b, tq=tm, window=SWA_WINDOW)
    cosm, sinm = _rope_tables(positions[0])

    xf = x[0]
    xb = xf.astype(BF16)
    for layer in range(depth):
        lam_init = 0.8 - 0.6 * math.exp(-0.3 * layer)
        w_abc, s_abc, w_d = _layer_weights(w_in[layer], 128 ** -0.5 * LOG2E, B_HEAD_DIM ** -0.5 * LOG2E,
                                           C_HEAD_DIM ** -0.5 * LOG2E)
        hs = proj_slots(xb, w_abc, s_abc, BF16, tm=tm, slots_per_step=8)
        hd = proj_slots(xb, w_d, jnp.ones((1, w_d.shape[1]), F32), F32, tm=tm, slots_per_step=8)

        nb = S // MOBA_BLOCK
        nbp = -(-nb // LANE) * LANE
        kmean = jnp.pad(moba_kmean(hs, 4, S), ((0, 0), (0, nbp - nb), (0, 0)))
        selb = moba_select(hs, 0, kmean, S, tq=min(1024, S))
        o_a = flash_pipe("A", hs, hs, hs, S=S, T=T, q_slot=lambda u: u, k_slot=lambda u: 4 + u,
                    v_slot=lambda u: 8 + u, bias=bias_a, selb=selb)
        sinks = ((swa_sinks[layer] - tab_b[T5_BUCKETS - 1]) * LOG2E).reshape(B_HEADS // 2, 2)
        o_b = swa_attention(hs, band_b, sinks, S=S, tq=tm, q_slot0=12, k_slot0=16, v_slot0=18)
        lam_params = jnp.stack([diff_lambda_q1[layer], diff_lambda_k1[layer],
                                diff_lambda_q2[layer], diff_lambda_k2[layer]])
        o_c = flash_pipe("C", hs, hs, hs, S=S, T=T, q_slot=lambda u: 20 + u, k_slot=lambda u: 24 + u,
                    v_slot=lambda u: 28 + u, bias=bias_c, lam_params=lam_params,
                    subln_g=diff_subln_g[layer][None, :], lam_init=lam_init)
        wqa, wqb, wkv = _mla_weights(mla_w_uq[layer], mla_w_ukv[layer])
        q_d, k_d, v_d = mla_prep(hd, mla_q_norm_g[layer][None, :], mla_kv_norm_g[layer][None, :],
                                 wqa, wqb, wkv, cosm, sinm, S=S, tm=tm)
        o_d = flash_pipe("D", q_d, k_d, v_d, S=S, T=T, q_slot=lambda u: u, k_slot=lambda u: u,
                    v_slot=lambda u: u)

        merged = merge(xb, (o_a, o_b, o_c, o_d),
                       w_gate[layer].astype(BF16), b_gate[layer][:, None, :], w_branch[layer].astype(BF16), tm=tm, tn=512)
        xf, xb = out_ln(xf, merged, w_out[layer].astype(BF16), ln1_g[layer][None, :],
                        ln1_b[layer][None, :], alpha=alpha, tm=tm)

        q_slots = proj_slots(xb, peer_w_q[layer].astype(BF16), jnp.ones((1, D), F32), F32,
                             tm=tm, slots_per_step=8)
        te = 512
        th_t, s1, e1, c_t = peer_route(q_slots, peer_sub_keys[layer], S=S, tq=tm, n_rows=te // PEER_N_KEYS)
        y = peer_experts(xb, peer_u[layer].astype(BF16), peer_v[layer].astype(BF16),
                         th_t, s1, e1, c_t, tq=tm, te=te)
        xf, xb = add_ln(xf, y, ln2_g[layer][None, :], ln2_b[layer][None, :], alpha=alpha, tm=tm)
    return xf[None]
```

```python
import functools
import math

import jax
import jax.numpy as jnp
from jax import lax
from jax.experimental import pallas as pl
from jax.experimental.pallas import tpu as pltpu

F32 = jnp.float32
BF16 = jnp.bfloat16
LANE = 128
NEG = -1e30
LOG2E = math.log2(math.e)

D_MODEL = 2048
A_HEADS = 4
MOBA_BLOCK = 256
MOBA_TOPK = 3
B_HEADS = 8
B_KV_HEADS = 2
B_HEAD_DIM = 64
SWA_WINDOW = 128
C_HEADS = 4
C_HEAD_DIM = 64
D_HEADS = 4
D_Q_LORA = 512
D_KV_LORA = 256
D_NOPE = 128
D_ROPE = 64
D_V = 128
ROPE_THETA = 10000.0
N_BRANCHES = 4
BRANCH_W = 512
T5_BUCKETS = 32
T5_MAX_EXACT = T5_BUCKETS // 2
T5_MAX_DISTANCE = 1024
PEER_HEADS = 8
PEER_N_KEYS = 128
PEER_D_KEY = 128
PEER_TOPK = 16
LN_EPS = 1e-5
RMS_EPS = 1e-6

VMEM_LIMIT_MB = 56


def _cparams(*sem):
    return pltpu.CompilerParams(dimension_semantics=sem,
                                vmem_limit_bytes=VMEM_LIMIT_MB * 1024 * 1024)


_NT = (((1,), (1,)), ((), ()))
_TN = (((0,), (0,)), ((), ()))


def _proj_kernel(x_ref, w_ref, s_ref, o_ref):
    acc = jnp.dot(x_ref[...], w_ref[...], preferred_element_type=F32) * s_ref[...]
    for c in range(o_ref.shape[0]):
        o_ref[c] = acc[:, c * LANE:(c + 1) * LANE].astype(o_ref.dtype)


def proj_slots(x_bf, w_bf, colscale, out_dtype, *, tm, slots_per_step):
    S, K = x_bf.shape
    N = w_bf.shape[1]
    tn = slots_per_step * LANE
    return pl.pallas_call(
        _proj_kernel,
        grid=(N // tn, S // tm),
        in_specs=[pl.BlockSpec((tm, K), lambda j, i: (i, 0)),
                  pl.BlockSpec((K, tn), lambda j, i: (0, j)),
                  pl.BlockSpec((1, tn), lambda j, i: (0, j))],
        out_specs=pl.BlockSpec((slots_per_step, tm, LANE), lambda j, i: (j, i, 0)),
        out_shape=jax.ShapeDtypeStruct((N // LANE, S, LANE), out_dtype),
        compiler_params=_cparams("parallel", "parallel"),
        name="proj_slots",
    )(x_bf, w_bf, colscale)


def _t5_tiles_kernel(tab_ref, o_ref, *, T, n_heads):
    dj = pl.program_id(0)
    row = lax.broadcasted_iota(jnp.int32, (T, T), 0)
    col = lax.broadcasted_iota(jnp.int32, (T, T), 1)
    bucket = _t5_bucket(jnp.maximum(row - col + dj * T, 0))
    for h in range(n_heads):
        last = tab_ref[T5_BUCKETS - 1, h]
        acc = jnp.zeros((T, T), F32)
        for b in range(T5_BUCKETS - 1):
            acc = jnp.where(bucket == b, (tab_ref[b, h] - last) * LOG2E, acc)
        o_ref[h, 0] = acc


def t5_tiles(tab, *, T, n_near):
    H = tab.shape[1]
    return pl.pallas_call(
        functools.partial(_t5_tiles_kernel, T=T, n_heads=H),
        grid=(n_near,),
        in_specs=[pl.BlockSpec(memory_space=pltpu.SMEM)],
        out_specs=pl.BlockSpec((H, 1, T, T), lambda d: (0, d, 0, 0)),
        out_shape=jax.ShapeDtypeStruct((H, n_near, T, T), F32),
        compiler_params=_cparams("parallel"),
        name="t5_tiles",
    )(tab)


def _n_near(T):
    return (T5_MAX_DISTANCE - 1 + T - 1) // T + 1


def _t5_bucket(n):
    nf = jnp.maximum(n, 1).astype(F32)
    large = T5_MAX_EXACT + (jnp.log(nf / T5_MAX_EXACT) / math.log(T5_MAX_DISTANCE / T5_MAX_EXACT)
                            * (T5_BUCKETS - T5_MAX_EXACT)).astype(jnp.int32)
    return jnp.where(n < T5_MAX_EXACT, n, jnp.minimum(large, T5_BUCKETS - 1))


def _t5_band_kernel(tab_ref, o_ref, *, window):
    h = pl.program_id(0)
    rows, cols = o_ref.shape
    dist = (lax.broadcasted_iota(jnp.int32, (rows, cols), 0)
            - lax.broadcasted_iota(jnp.int32, (rows, cols), 1) + window)
    bucket = _t5_bucket(jnp.maximum(dist, 0))
    last = tab_ref[T5_BUCKETS - 1, h]
    acc = jnp.zeros((rows, cols), F32)
    for b in range(T5_BUCKETS - 1):
        acc = jnp.where(bucket == b, (tab_ref[b, h] - last) * LOG2E, acc)
    o_ref[...] = jnp.where((dist >= 0) & (dist < window), acc, NEG)


def t5_band(tab, *, tq, window):
    H = tab.shape[1]
    cols = tq + 2 * window
    return pl.pallas_call(
        functools.partial(_t5_band_kernel, window=window),
        grid=(H,),
        in_specs=[pl.BlockSpec(memory_space=pltpu.SMEM)],
        out_specs=pl.BlockSpec((None, tq, cols), lambda h: (h, 0, 0)),
        out_shape=jax.ShapeDtypeStruct((H, tq, cols), F32),
        compiler_params=_cparams("parallel"),
        name="t5_band",
    )(tab)


def _swa_kernel(q_ref, k_ref, v_ref, band_ref, sink_ref, o_ref, *, window):
    u = pl.program_id(0)
    qi = pl.program_id(1)
    tq = q_ref.shape[0]
    nk = tq + window
    half = LANE // 2
    start = pl.multiple_of(jnp.maximum(qi * tq - window, 0), window)
    kb = k_ref[pl.ds(start, nk), :]
    vb = v_ref[pl.ds(start, nk), :]
    q = q_ref[...]
    z = jnp.zeros((tq, half), q.dtype)
    outs = []
    for c in range(2):
        qc = jnp.concatenate([q[:, c * half:(c + 1) * half], z], axis=1)
        band = jnp.where(qi == 0, band_ref[c, :, window:], band_ref[c, :, :nk])
        s = lax.dot_general(qc, kb, _NT, preferred_element_type=F32) + band
        sink = sink_ref[u, c]
        m = jnp.maximum(jnp.max(s, axis=-1, keepdims=True), sink)
        p = jnp.exp2(s - m)
        l = jnp.sum(p, axis=-1, keepdims=True) + jnp.exp2(sink - m)
        o = jnp.dot(p.astype(vb.dtype), vb, preferred_element_type=F32) / l
        outs.append(o[:, :half])
    o_ref[...] = jnp.concatenate(outs, axis=1).astype(o_ref.dtype)


def swa_attention(hs, band, sinks, *, S, tq, q_slot0, k_slot0, v_slot0):
    n_units = B_HEADS // 2
    per_kv = n_units // B_KV_HEADS
    cols = band.shape[-1]
    return pl.pallas_call(
        functools.partial(_swa_kernel, window=SWA_WINDOW),
        grid=(n_units, S // tq),
        in_specs=[pl.BlockSpec((None, tq, LANE), lambda u, i: (q_slot0 + u, i, 0)),
                  pl.BlockSpec((None, S, LANE), lambda u, i: (k_slot0 + u // per_kv, 0, 0)),
                  pl.BlockSpec((None, S, LANE), lambda u, i: (v_slot0 + u // per_kv, 0, 0)),
                  pl.BlockSpec((2, tq, cols), lambda u, i: (u, 0, 0)),
                  pl.BlockSpec(memory_space=pltpu.SMEM)],
        out_specs=pl.BlockSpec((None, tq, LANE), lambda u, i: (u, i, 0)),
        out_shape=jax.ShapeDtypeStruct((n_units, S, LANE), BF16),
        compiler_params=_cparams("parallel", "arbitrary"),
        name="swa",
    )(hs, hs, hs, band, sinks)


def _kmean_kernel(k_ref, o_ref, *, blk):
    k = k_ref[...].astype(F32)
    nb = k.shape[0] // blk
    o_ref[...] = k.reshape(nb, blk, LANE).sum(axis=1) * (1.0 / blk)


def moba_kmean(hs, k_slot0, S):
    nb = S // MOBA_BLOCK
    return pl.pallas_call(
        functools.partial(_kmean_kernel, blk=MOBA_BLOCK),
        grid=(A_HEADS,),
        in_specs=[pl.BlockSpec((None, S, LANE), lambda h: (k_slot0 + h, 0, 0))],
        out_specs=pl.BlockSpec((None, nb, LANE), lambda h: (h, 0, 0)),
        out_shape=jax.ShapeDtypeStruct((A_HEADS, nb, LANE), F32),
        compiler_params=_cparams("parallel"),
        name="moba_kmean",
    )(hs)


def _moba_select_kernel(q_ref, km_ref, o_ref, *, topk):
    tq = q_ref.shape[0]
    q = q_ref[...].astype(F32)
    gate = lax.dot_general(q, km_ref[...], _NT, precision=lax.Precision.HIGHEST,
                           preferred_element_type=F32)
    lane = lax.broadcasted_iota(jnp.int32, gate.shape, 1)
    row = lax.broadcasted_iota(jnp.int32, gate.shape, 0)
    own = pl.program_id(1) * (tq // MOBA_BLOCK) + row // MOBA_BLOCK
    nbp = gate.shape[1]
    gate = jnp.where(lane < own, gate, -jnp.inf)
    mask = jnp.where(lane == own, 0.0, NEG)
    for _ in range(topk):
        mx = jnp.max(gate, axis=-1, keepdims=True)
        idx = jnp.min(jnp.where(gate == mx, lane, nbp), axis=-1, keepdims=True)
        hit = lane == idx
        mask = jnp.where(hit, jnp.where(mx > -jnp.inf, 0.0, mask), mask)
        gate = jnp.where(hit, -jnp.inf, gate)
    o_ref[...] = mask.astype(o_ref.dtype)


def moba_select(hs, q_slot0, kmean_pad, S, *, tq):
    nbp = kmean_pad.shape[1]
    return pl.pallas_call(
        functools.partial(_moba_select_kernel, topk=MOBA_TOPK),
        grid=(A_HEADS, S // tq),
        in_specs=[pl.BlockSpec((None, tq, LANE), lambda h, i: (q_slot0 + h, i, 0)),
                  pl.BlockSpec((None, nbp, LANE), lambda h, i: (h, 0, 0))],
        out_specs=pl.BlockSpec((None, tq, nbp), lambda h, i: (h, i, 0)),
        out_shape=jax.ShapeDtypeStruct((A_HEADS, S, nbp), BF16),
        compiler_params=_cparams("parallel", "parallel"),
        name="moba_select",
    )(hs, kmean_pad)


def _flash_pipe_kernel(*refs, mode, T, n_near, lam_init):
    it = iter(refs)
    q_ref, k_ref, v_ref = next(it), next(it), next(it)
    bias_ref = next(it) if mode in "AC" else None
    selb_ref = next(it) if mode == "A" else None
    lam_ref, g_ref = (next(it), next(it)) if mode == "C" else (None, None)
    o_ref = next(it)
    m_scr, l_scr, acc_scr = next(it), next(it), next(it)
    rest = list(it)
    s_scr, p_scr = rest[:len(rest) // 2], rest[len(rest) // 2:]

    qi = pl.program_id(1)
    q = q_ref[...]
    half = LANE // 2
    if mode == "A":
        qs = [jnp.concatenate([q, selb_ref[...]], axis=1)]
    elif mode == "C":
        lane = lax.broadcasted_iota(jnp.int32, q.shape, 1)
        zq = jnp.zeros_like(q)
        qs = [jnp.where(lane < half, q, zq), jnp.where(lane >= half, q, zq)]
    else:
        qs = [q]
    n_sub = len(qs)

    def k_tile(j):
        kt = k_ref[pl.ds(pl.multiple_of(j * T, T), T), :]
        if mode == "A":
            n_blk = selb_ref.shape[-1]
            blk = lax.broadcasted_iota(jnp.int32, (T, n_blk), 1)
            row = lax.broadcasted_iota(jnp.int32, (T, n_blk), 0)
            own = j * (T // MOBA_BLOCK) + row // MOBA_BLOCK
            kt = jnp.concatenate([kt, jnp.where(blk == own, 1.0, 0.0).astype(kt.dtype)], axis=1)
        return kt

    def v_tile(j):
        return v_ref[pl.ds(pl.multiple_of(j * T, T), T), :]

    def scores(c, kt):
        return lax.dot_general(qs[c], kt, _NT, preferred_element_type=F32)

    kt0 = k_tile(0)
    for c in range(n_sub):
        s_scr[c][0] = scores(c, kt0)
        p_scr[c][1] = jnp.zeros((T, T), p_scr[c].dtype)
        m_scr[c] = jnp.full((T, LANE), NEG, F32)
        l_scr[c] = jnp.zeros((T, LANE), F32)
        acc_scr[c] = jnp.zeros(acc_scr.shape[1:], F32)

    def step(j, slot, near):
        nslot = 1 - slot
        kt_next = k_tile(jnp.minimum(j + 1, qi))
        vt_prev = v_tile(jnp.maximum(j - 1, 0))
        if near:
            dj = qi - j
            row = lax.broadcasted_iota(jnp.int32, (T, T), 0)
            col = lax.broadcasted_iota(jnp.int32, (T, T), 1)
            valid = row - col + dj * T >= 0
        for c in range(n_sub):
            pv = jnp.dot(p_scr[c][nslot], vt_prev, preferred_element_type=F32)
            s = s_scr[c][slot]
            s_scr[c][nslot] = scores(c, kt_next)
            if near:
                if bias_ref is not None:
                    s = s + bias_ref[0, dj]
                s = jnp.where(valid, s, NEG)
            m_prev = m_scr[c]
            m_new = jnp.maximum(m_prev, jnp.max(s, axis=-1, keepdims=True))
            alpha = jnp.exp2(m_prev - m_new)
            ps = [jnp.exp2(s[:, i * LANE:(i + 1) * LANE] - m_new) for i in range(T // LANE)]
            p_scr[c][slot] = jnp.concatenate(ps, axis=1).astype(p_scr[c].dtype)
            l_scr[c] = alpha * l_scr[c] + functools.reduce(lambda a, b: a + b, ps)
            acc_scr[c] = alpha * (acc_scr[c] + pv)
            m_scr[c] = m_new

    n_pairs = jnp.maximum(qi - n_near + 1, 0) // 2

    def far_pair(j0):
        step(j0, 0, False)
        step(j0 + 1, 1, False)

    def far_oct(i, _):
        for p in range(4):
            far_pair(8 * i + 2 * p)
        return 0

    n_octs = n_pairs // 4
    lax.fori_loop(0, n_octs, far_oct, 0)
    for p in range(3):
        @pl.when(4 * n_octs + p < n_pairs)
        def _():
            far_pair(8 * n_octs + 2 * p)
    for t in range(n_near + 1):
        jn = 2 * n_pairs + t

        @pl.when(jn <= qi)
        def _():
            step(jn, t % 2, True)

    last = lax.rem(qi, 2)
    vt = v_tile(qi)
    outs = []
    for c in range(n_sub):
        acc = acc_scr[c] + jnp.dot(p_scr[c][last], vt, preferred_element_type=F32)
        outs.append(acc / jnp.sum(l_scr[c], axis=-1, keepdims=True))
    if mode == "C":
        lp = lam_ref[...]
        lam = (jnp.exp(jnp.sum(lp[0:1] * lp[1:2], axis=-1, keepdims=True))
               - jnp.exp(jnp.sum(lp[2:3] * lp[3:4], axis=-1, keepdims=True)) + lam_init)
        o = outs[0] - lam * outs[1]
        o = o * lax.rsqrt(jnp.mean(o * o, axis=-1, keepdims=True) + RMS_EPS) * g_ref[...]
        o = o * (1.0 - lam_init)
    else:
        o = outs[0]
    o_ref[...] = o.astype(o_ref.dtype)


def flash_pipe(mode, q_arr, k_arr, v_arr, *, S, T, q_slot, k_slot, v_slot, n_units=4, bias=None,
               selb=None, lam_params=None, subln_g=None, lam_init=0.0):
    dq = q_arr.shape[-1]
    dk = k_arr.shape[-1]
    dv = v_arr.shape[-1]
    n_near = 1 if mode == "D" else _n_near(T)
    n_sub = 2 if mode == "C" else 1
    in_specs = [pl.BlockSpec((None, T, dq), lambda u, i: (q_slot(u), i, 0)),
                pl.BlockSpec((None, S, dk), lambda u, i: (k_slot(u), 0, 0)),
                pl.BlockSpec((None, S, dv), lambda u, i: (v_slot(u), 0, 0))]
    args = [q_arr, k_arr, v_arr]
    if bias is not None:
        in_specs.append(pl.BlockSpec((None,) + bias.shape[1:], lambda u, i: (u, 0, 0, 0, 0)))
        args.append(bias)
    if selb is not None:
        in_specs.append(pl.BlockSpec((None, T, selb.shape[-1]), lambda u, i: (u, i, 0)))
        args.append(selb)
    if lam_params is not None:
        in_specs.append(pl.BlockSpec(lam_params.shape, lambda u, i: (0, 0)))
        in_specs.append(pl.BlockSpec(subln_g.shape, lambda u, i: (0, 0)))
        args += [lam_params, subln_g]
    return pl.pallas_call(
        functools.partial(_flash_pipe_kernel, mode=mode, T=T, n_near=n_near, lam_init=lam_init),
        grid=(n_units, S // T),
        in_specs=in_specs,
        out_specs=pl.BlockSpec((None, T, LANE), lambda u, i: (u, i, 0)),
        out_shape=jax.ShapeDtypeStruct((n_units, S, LANE), BF16),
        scratch_shapes=([pltpu.VMEM((n_sub, T, LANE), F32), pltpu.VMEM((n_sub, T, LANE), F32),
                         pltpu.VMEM((n_sub, T, dv), F32)]
                        + [pltpu.VMEM((2, T, T), F32)] * n_sub + [pltpu.VMEM((2, T, T), BF16)] * n_sub),
        compiler_params=_cparams("parallel", "arbitrary"),
        name="flashp_" + mode,
    )(*args)


def _rms(x, g):
    return x * lax.rsqrt(jnp.mean(x * x, axis=-1, keepdims=True) + RMS_EPS) * g


def _mla_prep_kernel(hd_ref, gq_ref, gkv_ref, wqa_ref, wqb_ref, wkv_ref, cos_ref, sin_ref,
                     qo_ref, ko_ref, vo_ref, *, scale):
    cosm = cos_ref[...]
    sinm = sin_ref[...]
    cq = jnp.concatenate([hd_ref[c] for c in range(4)], axis=1)
    cqn = _rms(cq, gq_ref[...]).astype(BF16)
    qa = jnp.dot(cqn, wqa_ref[...], preferred_element_type=F32)
    qb = jnp.dot(cqn, wqb_ref[...], preferred_element_type=F32)
    w = 2 * LANE
    for h in range(D_HEADS):
        qe = (qa[:, h * w:(h + 1) * w] * cosm + qb[:, h * w:(h + 1) * w] * sinm) * scale
        qo_ref[h] = qe.astype(qo_ref.dtype)
    ckv = jnp.concatenate([hd_ref[4], hd_ref[5]], axis=1)
    ckvn = _rms(ckv, gkv_ref[...]).astype(BF16)
    kv = jnp.dot(ckvn, wkv_ref[...], preferred_element_type=F32)
    kpe = hd_ref[6] * cosm[:, LANE:] + hd_ref[7] * sinm[:, LANE:]
    for h in range(D_HEADS):
        ko_ref[h] = jnp.concatenate([kv[:, h * LANE:(h + 1) * LANE], kpe], axis=1).astype(ko_ref.dtype)
        vo_ref[h] = kv[:, (D_HEADS + h) * LANE:(D_HEADS + h + 1) * LANE].astype(vo_ref.dtype)


def mla_prep(hd, gq, gkv, wqa, wqb, wkv, cosm, sinm, *, S, tm):
    w = 2 * LANE
    full = lambda a: pl.BlockSpec(a.shape, lambda i: (0,) * a.ndim)
    return pl.pallas_call(
        functools.partial(_mla_prep_kernel, scale=(D_NOPE + D_ROPE) ** -0.5 * LOG2E),
        grid=(S // tm,),
        in_specs=[pl.BlockSpec((8, tm, LANE), lambda i: (0, i, 0)),
                  full(gq), full(gkv), full(wqa), full(wqb), full(wkv),
                  pl.BlockSpec((tm, w), lambda i: (i, 0)),
                  pl.BlockSpec((tm, w), lambda i: (i, 0))],
        out_specs=[pl.BlockSpec((D_HEADS, tm, w), lambda i: (0, i, 0)),
                   pl.BlockSpec((D_HEADS, tm, w), lambda i: (0, i, 0)),
                   pl.BlockSpec((D_HEADS, tm, LANE), lambda i: (0, i, 0))],
        out_shape=[jax.ShapeDtypeStruct((D_HEADS, S, w), BF16),
                   jax.ShapeDtypeStruct((D_HEADS, S, w), BF16),
                   jax.ShapeDtypeStruct((D_HEADS, S, LANE), BF16)],
        compiler_params=_cparams("parallel"),
        name="mla_prep",
    )(hd, gq, gkv, wqa, wqb, wkv, cosm, sinm)


def _merge_kernel(x_ref, oa_ref, ob_ref, oc_ref, od_ref, wg_ref, bg_ref, wbr_ref, o_ref):
    x = x_ref[...]
    acc = None
    for n, o in enumerate((oa_ref, ob_ref, oc_ref, od_ref)):
        g = jnp.dot(x, wg_ref[n], preferred_element_type=F32) + bg_ref[n]
        ocat = jnp.concatenate([o[c] for c in range(4)], axis=1)
        b = jnp.dot(ocat, wbr_ref[n], preferred_element_type=F32)
        t = jax.nn.sigmoid(g) * b
        acc = t if acc is None else acc + t
    o_ref[...] = acc.astype(o_ref.dtype)


def merge(x_bf, outs, wg, bg, wbr, *, tm, tn):
    S, D = x_bf.shape
    o_spec = pl.BlockSpec((4, tm, LANE), lambda j, i: (0, i, 0))
    return pl.pallas_call(
        _merge_kernel,
        grid=(D // tn, S // tm),
        in_specs=[pl.BlockSpec((tm, D), lambda j, i: (i, 0)), o_spec, o_spec, o_spec, o_spec,
                  pl.BlockSpec((N_BRANCHES, D, tn), lambda j, i: (0, 0, j)),
                  pl.BlockSpec((N_BRANCHES, 1, tn), lambda j, i: (0, 0, j)),
                  pl.BlockSpec((N_BRANCHES, BRANCH_W, tn), lambda j, i: (0, 0, j))],
        out_specs=pl.BlockSpec((tm, tn), lambda j, i: (i, j)),
        out_shape=jax.ShapeDtypeStruct((S, D), BF16),
        compiler_params=_cparams("parallel", "parallel"),
        name="merge",
    )(x_bf, *outs, wg, bg, wbr)


def _layer_norm(y, g, b):
    mu = jnp.mean(y, axis=-1, keepdims=True)
    var = jnp.mean(jnp.square(y - mu), axis=-1, keepdims=True)
    return (y - mu) * lax.rsqrt(var + LN_EPS) * g + b


def _out_ln_kernel(x_ref, m_ref, w_ref, g_ref, b_ref, of_ref, ob_ref, *, alpha):
    y = alpha * x_ref[...] + jnp.dot(m_ref[...], w_ref[...], preferred_element_type=F32)
    o = _layer_norm(y, g_ref[...], b_ref[...])
    of_ref[...] = o
    ob_ref[...] = o.astype(ob_ref.dtype)


def out_ln(x, merged, w_out, g, b, *, alpha, tm):
    S, D = x.shape
    row = pl.BlockSpec((tm, D), lambda i: (i, 0))
    vec = pl.BlockSpec((1, D), lambda i: (0, 0))
    return pl.pallas_call(
        functools.partial(_out_ln_kernel, alpha=alpha),
        grid=(S // tm,),
        in_specs=[row, row, pl.BlockSpec((D, D), lambda i: (0, 0)), vec, vec],
        out_specs=[row, row],
        out_shape=[jax.ShapeDtypeStruct((S, D), F32), jax.ShapeDtypeStruct((S, D), BF16)],
        compiler_params=_cparams("parallel"),
        name="out_ln",
    )(x, merged, w_out, g, b)


def _add_ln_kernel(x_ref, y_ref, g_ref, b_ref, of_ref, ob_ref, *, alpha):
    o = _layer_norm(alpha * x_ref[...] + y_ref[...], g_ref[...], b_ref[...])
    of_ref[...] = o
    ob_ref[...] = o.astype(ob_ref.dtype)


def add_ln(x, y, g, b, *, alpha, tm):
    S, D = x.shape
    row = pl.BlockSpec((tm, D), lambda i: (i, 0))
    vec = pl.BlockSpec((1, D), lambda i: (0, 0))
    return pl.pallas_call(
        functools.partial(_add_ln_kernel, alpha=alpha),
        grid=(S // tm,),
        in_specs=[row, row, vec, vec],
        out_specs=[row, row],
        out_shape=[jax.ShapeDtypeStruct((S, D), F32), jax.ShapeDtypeStruct((S, D), BF16)],
        compiler_params=_cparams("parallel"),
        name="add_ln",
    )(x, y, g, b)


def _top_sorted(s, k):
    vals = []
    cur = s
    for r in range(k):
        m = jnp.max(cur, axis=0, keepdims=True)
        vals.append(m)
        if r + 1 < k:
            cur = jnp.where(cur == m, -jnp.inf, cur)
    return vals


def _peer_route_kernel(q_ref, keys_ref, th_ref, s1_ref, e1_ref, c_ref):
    K = PEER_TOPK

    def head(h, _):
        s0 = lax.dot_general(keys_ref[h, 0], q_ref[2 * h], _NT, precision=lax.Precision.HIGHEST,
                             preferred_element_type=F32)
        s1 = lax.dot_general(keys_ref[h, 1], q_ref[2 * h + 1], _NT, precision=lax.Precision.HIGHEST,
                             preferred_element_type=F32)
        v0 = _top_sorted(s0, K)
        v1 = jnp.concatenate(_top_sorted(s1, K), axis=0)
        hk = K // 2
        cands = ([v0[0] + v1[:hk], v0[0] + v1[hk:]]
                 + [v0[a] + v1[:hk] for a in range(1, hk)]
                 + [jnp.concatenate(v0[hk:], axis=0) + v1[0:1]])
        sums = cands
        z = None
        for r in range(K):
            m = functools.reduce(jnp.maximum, cands)
            m = jnp.max(m, axis=0, keepdims=True)
            if r == 0:
                top = m
                z = jnp.ones_like(m)
            else:
                z = z + jnp.exp(m - top)
            if r + 1 < K:
                cands = [jnp.where(c == m, -jnp.inf, c) for c in cands]
        c = jnp.exp(s0 - v0[0]) / z
        inf = jnp.inf

        def smallest(sm, vals):
            return jnp.min(jnp.where(sm >= m, vals, inf), axis=0, keepdims=True)

        thetas = [jnp.minimum(smallest(sums[0], v1[:hk]), smallest(sums[1], v1[hk:]))]
        thetas += [smallest(sums[a + 1], v1[:hk]) for a in range(1, hk)]
        theta_hi = jnp.where(sums[hk + 1] >= m, v1[0:1], inf)
        thetas += [theta_hi[a:a + 1] for a in range(K - hk)]
        theta = jnp.full_like(s0, inf)
        for a in range(K):
            theta = jnp.where(s0 == v0[a], thetas[a], theta)
        n_rows = th_ref.shape[2]
        for n in range(th_ref.shape[1]):
            th_ref[h, n] = theta[n * n_rows:(n + 1) * n_rows]
            c_ref[h, n] = c[n * n_rows:(n + 1) * n_rows]
        s1_ref[h] = s1
        e1_ref[h] = jnp.exp(s1 - v1[0:1])
        return 0

    lax.fori_loop(0, PEER_HEADS, head, 0, unroll=4)


def peer_route(q_slots, sub_keys, *, S, tq, n_rows):
    arr = jax.ShapeDtypeStruct((PEER_HEADS, PEER_N_KEYS, S), F32)
    spec = pl.BlockSpec((PEER_HEADS, PEER_N_KEYS, tq), lambda i: (0, 0, i))
    n_tiles = PEER_N_KEYS // n_rows
    arr_t = jax.ShapeDtypeStruct((PEER_HEADS, n_tiles, n_rows, S), F32)
    spec_t = pl.BlockSpec((PEER_HEADS, n_tiles, n_rows, tq), lambda i: (0, 0, 0, i))
    return pl.pallas_call(
        _peer_route_kernel,
        grid=(S // tq,),
        in_specs=[pl.BlockSpec((2 * PEER_HEADS, tq, LANE), lambda i: (0, i, 0)),
                  pl.BlockSpec(sub_keys.shape, lambda i: (0, 0, 0, 0))],
        out_specs=[spec_t, spec, spec, spec_t],
        out_shape=[arr_t, arr, arr, arr_t],
        compiler_params=_cparams("parallel"),
        name="peer_route",
    )(q_slots, sub_keys)


def _peer_expert_kernel(x_ref, u_ref, v_ref, tha_ref, thb_ref, ca_ref, cb_ref, s1_ref, e1_ref,
                        y_ref, ha_scr, hb_scr, wa_scr, wb_scr):
    k = pl.program_id(1)
    te, tq = ha_scr.shape
    n_rows = te // PEER_N_KEYS
    D = y_ref.shape[1]
    nc = D // n_rows

    @pl.when(k == 0)
    def _():
        y_ref[...] = jnp.zeros_like(y_ref)
        hb_scr[...] = jnp.zeros_like(hb_scr)
        wa_scr[...] = jnp.zeros_like(wa_scr)

    def phase(u_lo, h_out, w_prev, h_prev, w_out, th_ref, c_ref):
        rb = 32

        def y_chunk(r):
            cols = pl.ds(r * nc, nc)
            y_ref[:, cols] += lax.dot_general(w_prev[...], v_ref[pl.ds(u_lo, te), cols], _TN,
                                              preferred_element_type=F32)

        def h_half(n):
            toks = pl.ds(n * (tq // 2), tq // 2)
            h_out[:, toks] = lax.dot_general(u_ref[pl.ds(u_lo, te), :], x_ref[toks, :], _NT,
                                             preferred_element_type=F32)

        def gate_block(tc, sb):
            tl = pl.ds(tc * LANE, LANE)
            keys = pl.ds(sb * rb, rb)
            gs = [None] * n_rows
            for h in range(PEER_HEADS):
                s1b = s1_ref[h, keys, tl]
                e1b = e1_ref[h, keys, tl]
                for r in range(n_rows):
                    t = jnp.where(s1b >= th_ref[h, pl.ds(r, 1), tl], e1b, 0.0) * c_ref[h, pl.ds(r, 1), tl]
                    gs[r] = t if gs[r] is None else gs[r] + t
            for r in range(n_rows):
                rows = pl.ds(r * PEER_N_KEYS + sb * rb, rb)
                hp = h_prev[rows, tl]
                act = 0.5 * hp * (1.0 + lax.erf(hp * math.sqrt(0.5)))
                w_out[rows, tl] = (gs[r] * act).astype(w_out.dtype)

        blocks = [(tc, sb) for tc in range(tq // LANE) for sb in range(PEER_N_KEYS // rb)]
        pieces = [functools.partial(y_chunk, r) for r in range(n_rows)] + [functools.partial(h_half, n) for n in range(2)]
        weights = [1] * n_rows + [2, 2]
        per = len(blocks) // sum(weights)
        b0 = 0
        for piece, wgt in zip(pieces, weights):
            piece()
            for tc, sb in blocks[b0:b0 + per * wgt]:
                gate_block(tc, sb)
            b0 += per * wgt
        for tc, sb in blocks[b0:]:
            gate_block(tc, sb)

    phase(0, ha_scr, wa_scr, hb_scr, wb_scr, tha_ref, ca_ref)
    phase(te, hb_scr, wb_scr, ha_scr, wa_scr, thb_ref, cb_ref)


def peer_experts(x_bf, u_bf, v_bf, th_t, s1, e1, c_t, *, tq, te):
    S, D = x_bf.shape
    E = u_bf.shape[0]
    n_tiles = E // te
    n_blocks = n_tiles // 2
    n_rows = te // PEER_N_KEYS
    assert th_t.shape == c_t.shape == (PEER_HEADS, n_tiles, n_rows, S)
    rt = pl.BlockSpec((PEER_HEADS, PEER_N_KEYS, tq), lambda i, k: (0, 0, i))
    row_a = pl.BlockSpec((PEER_HEADS, None, n_rows, tq), lambda i, k: (0, jnp.maximum(2 * k - 1, 0), 0, i))
    row_b = pl.BlockSpec((PEER_HEADS, None, n_rows, tq),
                         lambda i, k: (0, jnp.minimum(2 * k, n_tiles - 1), 0, i))
    return pl.pallas_call(
        _peer_expert_kernel,
        grid=(S // tq, n_blocks + 1),
        in_specs=[pl.BlockSpec((tq, D), lambda i, k: (i, 0)),
                  pl.BlockSpec((2 * te, D), lambda i, k: (jnp.minimum(k, n_blocks - 1), 0)),
                  pl.BlockSpec((2 * te, D), lambda i, k: (jnp.maximum(k - 1, 0), 0)),
                  row_a, row_b, row_a, row_b, rt, rt],
        out_specs=pl.BlockSpec((tq, D), lambda i, k: (i, 0)),
        out_shape=jax.ShapeDtypeStruct((S, D), F32),
        scratch_shapes=[pltpu.VMEM((te, tq), F32), pltpu.VMEM((te, tq), F32),
                        pltpu.VMEM((te, tq), BF16), pltpu.VMEM((te, tq), BF16)],
        compiler_params=_cparams("parallel", "arbitrary"),
        name="peer_experts",
    )(x_bf, u_bf, v_bf, th_t, th_t, c_t, c_t, s1, e1)


def _pad_cols(w, width):
    return jnp.pad(w, ((0, 0), (0, width - w.shape[1])))


def _rot_cols(w):
    h = w.shape[1] // 2
    return jnp.concatenate([-w[:, h:], w[:, :h]], axis=1)


def _split_w_in(w):
    widths = (512, 512, 512, 512, 128, 128, 512, 512, 512, D_Q_LORA, D_KV_LORA, D_ROPE)
    out, o = [], 0
    for n in widths:
        out.append(w[:, o:o + n])
        o += n
    return out


def _layer_weights(w_in, a_scale, b_scale, c_scale):
    (a_q, a_k, a_v, b_q, b_k, b_v, c_q, c_k, c_v, d_cq, d_ckv, d_kr) = _split_w_in(w_in)
    b_k0, b_k1 = b_k[:, :64], b_k[:, 64:]
    b_v0, b_v1 = b_v[:, :64], b_v[:, 64:]
    w_abc = jnp.concatenate(
        [a_q, a_k, a_v,
         b_q, _pad_cols(b_k0, LANE), _pad_cols(b_k1, LANE), _pad_cols(b_v0, LANE), _pad_cols(b_v1, LANE),
         c_q, c_k, c_v], axis=1).astype(BF16)
    ones = lambda n: jnp.ones((n,), F32)
    s_abc = jnp.concatenate([ones(512) * a_scale, ones(1024),
                             ones(512) * b_scale, ones(512),
                             ones(512) * c_scale, ones(1024)])[None, :]
    w_d = jnp.concatenate([d_cq, d_ckv, _pad_cols(d_kr, LANE), _pad_cols(_rot_cols(d_kr), LANE)],
                          axis=1).astype(BF16)
    return w_abc, s_abc, w_d


def _mla_weights(w_uq, w_ukv):
    wq = w_uq.reshape(D_Q_LORA, D_HEADS, D_NOPE + D_ROPE)
    nope, pe = wq[..., :D_NOPE], wq[..., D_NOPE:]
    pe_rot = jnp.concatenate([-pe[..., D_ROPE // 2:], pe[..., :D_ROPE // 2]], axis=-1)
    z64 = jnp.zeros((D_Q_LORA, D_HEADS, LANE - D_ROPE), F32)
    z128 = jnp.zeros((D_Q_LORA, D_HEADS, D_NOPE), F32)
    wqa = jnp.concatenate([nope, pe, z64], axis=-1).reshape(D_Q_LORA, -1).astype(BF16)
    wqb = jnp.concatenate([z128, pe_rot, z64], axis=-1).reshape(D_Q_LORA, -1).astype(BF16)
    wkv = w_ukv.reshape(D_KV_LORA, D_HEADS, D_NOPE + D_V)
    wkv = jnp.concatenate([wkv[..., :D_NOPE].reshape(D_KV_LORA, -1),
                           wkv[..., D_NOPE:].reshape(D_KV_LORA, -1)], axis=1).astype(BF16)
    return wqa, wqb, wkv


def _rope_tables(positions):
    half = D_ROPE // 2
    inv = ROPE_THETA ** (-jnp.arange(half, dtype=F32) / half)
    ang = positions.astype(F32)[..., None] * inv
    cos, sin = jnp.cos(ang), jnp.sin(ang)
    S = positions.shape[0]
    one = jnp.ones((S, D_NOPE), F32)
    z = jnp.zeros((S, LANE - D_ROPE), F32)
    cosm = jnp.concatenate([one, cos, cos, z], axis=1)
    sinm = jnp.concatenate([jnp.zeros((S, D_NOPE), F32), sin, sin, z], axis=1)
    return cosm, sinm


def kernel(x, positions, t5_table, w_in, w_gate, b_gate, w_branch, w_out, swa_sinks, diff_lambda_q1, diff_lambda_k1, diff_lambda_q2, diff_lambda_k2, diff_subln_g, mla_q_norm_g, mla_kv_norm_g, mla_w_uq, mla_w_ukv, ln1_g, ln1_b, peer_w_q, peer_sub_keys, peer_u, peer_v, ln2_g, ln2_b):
    B, S, D = x.shape
    assert B == 1 and D == D_MODEL and S % MOBA_BLOCK == 0
    depth = w_in.shape[0]
    alpha = (2.0 * depth) ** 0.25
    tm = min(512, S)
    T = min(512, S)

    tab_a = t5_table[:, :A_HEADS]
    tab_b = t5_table[:, A_HEADS:A_HEADS + B_HEADS]
    tab_c = t5_table[:, A_HEADS + B_HEADS:]
    bias_a = t5_tiles(tab_a, T=T, n_near=_n_near(T) + 1)[:, None]
    bias_c = t5_tiles(tab_c, T=T, n_near=_n_near(T) + 1)[:, None]
    band_b = t5_band(tab_b, tq=tm, window=SWA_WINDOW)
    cosm, sinm = _rope_tables(positions[0])

    xf = x[0]
    xb = xf.astype(BF16)
    for layer in range(depth):
        lam_init = 0.8 - 0.6 * math.exp(-0.3 * layer)
        w_abc, s_abc, w_d = _layer_weights(w_in[layer], 128 ** -0.5 * LOG2E, B_HEAD_DIM ** -0.5 * LOG2E,
                                           C_HEAD_DIM ** -0.5 * LOG2E)
        hs = proj_slots(xb, w_abc, s_abc, BF16, tm=tm, slots_per_step=8)
        hd = proj_slots(xb, w_d, jnp.ones((1, w_d.shape[1]), F32), F32, tm=tm, slots_per_step=8)

        nb = S // MOBA_BLOCK
        nbp = -(-nb // LANE) * LANE
        kmean = jnp.pad(moba_kmean(hs, 4, S), ((0, 0), (0, nbp - nb), (0, 0)))
        selb = moba_select(hs, 0, kmean, S, tq=min(1024, S))
        o_a = flash_pipe("A", hs, hs, hs, S=S, T=T, q_slot=lambda u: u, k_slot=lambda u: 4 + u,
                    v_slot=lambda u: 8 + u, bias=bias_a, selb=selb)
        sinks = ((swa_sinks[layer] - tab_b[T5_BUCKETS - 1]) * LOG2E).reshape(B_HEADS // 2, 2)
        o_b = swa_attention(hs, band_b, sinks, S=S, tq=tm, q_slot0=12, k_slot0=16, v_slot0=18)
        lam_params = jnp.stack([diff_lambda_q1[layer], diff_lambda_k1[layer],
                                diff_lambda_q2[layer], diff_lambda_k2[layer]])
        o_c = flash_pipe("C", hs, hs, hs, S=S, T=T, q_slot=lambda u: 20 + u, k_slot=lambda u: 24 + u,
                    v_slot=lambda u: 28 + u, bias=bias_c, lam_params=lam_params,
                    subln_g=diff_subln_g[layer][None, :], lam_init=lam_init)
        wqa, wqb, wkv = _mla_weights(mla_w_uq[layer], mla_w_ukv[layer])
        q_d, k_d, v_d = mla_prep(hd, mla_q_norm_g[layer][None, :], mla_kv_norm_g[layer][None, :],
                                 wqa, wqb, wkv, cosm, sinm, S=S, tm=tm)
        o_d = flash_pipe("D", q_d, k_d, v_d, S=S, T=T, q_slot=lambda u: u, k_slot=lambda u: u,
                    v_slot=lambda u: u)

        merged = merge(xb, (o_a, o_b, o_c, o_d),
                       w_gate[layer].astype(BF16), b_gate[layer][:, None, :], w_branch[layer].astype(BF16), tm=tm, tn=512)
        xf, xb = out_ln(xf, merged, w_out[layer].astype(BF16), ln1_g[layer][None, :],
                        ln1_b[layer][None, :], alpha=alpha, tm=tm)

        q_slots = proj_slots(xb, peer_w_q[layer].astype(BF16), jnp.ones((1, D), F32), F32,
                             tm=tm, slots_per_step=8)
        te = 512
        th_t, s1, e1, c_t = peer_route(q_slots, peer_sub_keys[layer], S=S, tq=tm, n_rows=te // PEER_N_KEYS)
        y = peer_experts(xb, peer_u[layer].astype(BF16), peer_v[layer].astype(BF16),
                         th_t, s1, e1, c_t, tq=tm, te=te)
        xf, xb = add_ln(xf, y, ln2_g[layer][None, :], ln2_b[layer][None, :], alpha=alpha, tm=tm)
    return xf[None]
```
